```python
import math
import jax, jax.numpy as jnp
from jax import lax
import numpy as np

D_MODEL = 1024
BATCH = 4
SEQ = 8192
DEPTH = 1
DEC_BATCH = 128
DEC_SEQ = 8
PAST_LEN = 16384
PAGE_SIZE = 128

MIX_WIDTH = D_MODEL
MLA_V = 64
MLA_NOPE = 64
MLA_ROPE = 32
MLA_HEADS = (MIX_WIDTH // 2) // MLA_V
Q_LORA = 3 * D_MODEL // 8
KV_LORA = D_MODEL // 4
FOX_DIM = 64
FOX_HEADS = (MIX_WIDTH - MLA_HEADS * MLA_V) // FOX_DIM
FOX_W = FOX_HEADS * FOX_DIM
D_FF = 4 * D_MODEL
N_IN = Q_LORA + KV_LORA + MLA_ROPE + 3 * FOX_W + FOX_HEADS
Q_BLOCK = 128
ROPE_THETA = 10000.0
EPS = 1e-6
NEG_INF = -1e30

kernel_name = "hymba_mla_fox_adaln_decode_step"


def rms_norm(x, g):
    x32 = x.astype(jnp.float32)
    y = x32 * lax.rsqrt(jnp.mean(x32 * x32, axis=-1, keepdims=True) + EPS)
    return (y * g.astype(jnp.float32)).astype(x.dtype)


def apply_rope(x, pos):
    half = x.shape[-1] // 2
    inv_freq = ROPE_THETA ** (-jnp.arange(half, dtype=jnp.float32) / half)
    ang = pos.astype(jnp.float32)[:, None] * inv_freq[None, :]
    cos = jnp.cos(ang)[None, :, None, :].astype(x.dtype)
    sin = jnp.sin(ang)[None, :, None, :].astype(x.dtype)
    x1, x2 = x[..., :half], x[..., half:]
    return jnp.concatenate([x1 * cos - x2 * sin, x1 * sin + x2 * cos], axis=-1)


def gather_pages(pool, page_table):
    g = pool[page_table]
    return g.reshape((g.shape[0], g.shape[1] * g.shape[2]) + g.shape[3:])


def attend(q, k, v, q_pos, k_pos, scale, q_bias=None, k_bias=None, q_shared=None, k_shared=None):
    b, sq, h, _ = q.shape
    blk = min(Q_BLOCK, sq)
    n_blk = -(-sq // blk)
    pad = n_blk * blk - sq

    def to_blocks(a):
        a = jnp.pad(a, [(0, 0), (0, pad)] + [(0, 0)] * (a.ndim - 2))
        return jnp.swapaxes(a.reshape((b, n_blk, blk) + a.shape[2:]), 0, 1)

    xs = [to_blocks(q), jnp.pad(q_pos, (0, pad), mode="edge").reshape(n_blk, blk)]
    if q_shared is not None:
        xs.append(to_blocks(q_shared))
    if q_bias is not None:
        xs.append(to_blocks(q_bias))
        k_bias_t = jnp.swapaxes(k_bias.astype(jnp.float32), 1, 2)[:, :, None, :]

    def one_block(args):
        qb, pb = args[0], args[1]
        s = jnp.einsum("bqhd,bkhd->bhqk", qb, k).astype(jnp.float32)
        i = 2
        if q_shared is not None:
            s = s + jnp.einsum("bqhr,bkr->bhqk", args[i], k_shared).astype(jnp.float32)
            i += 1
        s = s * scale
        if q_bias is not None:
            s = s + jnp.swapaxes(args[i].astype(jnp.float32), 1, 2)[..., None] - k_bias_t
        mask = k_pos[None, :] <= pb[:, None]
        s = jnp.where(mask[None, None], s, NEG_INF)
        p = jax.nn.softmax(s, axis=-1)
        return jnp.einsum("bhqk,bkhd->bqhd", p.astype(v.dtype), v)

    out = lax.map(one_block, tuple(xs))
    out = jnp.swapaxes(out, 0, 1).reshape((b, n_blk * blk) + out.shape[3:])
    return out[:, :sq]


def layer_step(x, c, q_pos, k_pos, past, w):
    (w_ada, b_ada, g_attn, g_mlp, w_in, b_f, g_qa, w_qb, g_qn, g_qr, g_kva, g_kr,
     w_kvb, g_kn, g_fq, g_fk, w_o, w_mlp1, w_mlp2) = w
    b, s, _ = x.shape
    mod = jax.nn.silu(c) @ w_ada + b_ada
    shift_a, scale_a, gate_a, shift_m, scale_m, gate_m = [m[:, None, :] for m in jnp.split(mod, 6, axis=-1)]

    h = rms_norm(x, g_attn) * (1 + scale_a) + shift_a
    z = h @ w_in
    cuts = np.cumsum([Q_LORA, KV_LORA, MLA_ROPE, FOX_W, FOX_W, FOX_W]).tolist()
    q_lat, kv_lat, kr_raw, fq, fk, fv, f_logit = jnp.split(z, cuts, axis=-1)

    qm = jnp.einsum("bsc,chd->bshd", rms_norm(q_lat, g_qa), w_qb)
    q_nope = rms_norm(qm[..., :MLA_NOPE], g_qn)
    q_rope = apply_rope(rms_norm(qm[..., MLA_NOPE:], g_qr), q_pos)
    ckv_new = rms_norm(kv_lat, g_kva)
    krope_new = apply_rope(rms_norm(kr_raw, g_kr)[:, :, None, :], q_pos)[:, :, 0, :]

    fq = rms_norm(fq.reshape(b, s, FOX_HEADS, FOX_DIM), g_fq)
    fk_new = rms_norm(fk.reshape(b, s, FOX_HEADS, FOX_DIM), g_fk)
    fv_new = fv.reshape(b, s, FOX_HEADS, FOX_DIM)
    logf_new = jax.nn.log_sigmoid((f_logit + b_f).astype(jnp.float32)).astype(x.dtype)

    new = (ckv_new, krope_new, fk_new, fv_new, logf_new)
    if past is None:
        ckv_all, krope_all, fk_all, fv_all, logf_all = new
    else:
        ckv_all, krope_all, fk_all, fv_all, logf_all = [
            jnp.concatenate([p_, n_.astype(p_.dtype)], axis=1) for p_, n_ in zip(past, new)]
    sk = ckv_all.shape[1]

    k_nope = rms_norm(jnp.einsum("bsc,chd->bshd", ckv_all, w_kvb[:, :, :MLA_NOPE]), g_kn)
    v_mla = jnp.einsum("bsc,chd->bshd", ckv_all, w_kvb[:, :, MLA_NOPE:])
    o_mla = attend(q_nope, k_nope, v_mla, q_pos, k_pos, (MLA_NOPE + MLA_ROPE) ** -0.5,
                   q_shared=q_rope, k_shared=krope_all)

    cum = jnp.cumsum(logf_all.astype(jnp.float32), axis=1)
    o_fox = attend(fq, fk_all, fv_all, q_pos, k_pos, FOX_DIM ** -0.5,
                   q_bias=cum[:, sk - s:], k_bias=cum)

    o = jnp.concatenate([o_mla.reshape(b, s, -1), o_fox.reshape(b, s, -1)], axis=-1) @ w_o
    x = x + gate_a * o

    h = rms_norm(x, g_mlp) * (1 + scale_m) + shift_m
    x = x + gate_m * (jnp.square(jax.nn.relu(h @ w_mlp1)) @ w_mlp2)
    return x, new


def setup_inputs(seed: int = 0) -> dict:
    key = jax.random.key(seed)
    ks = jax.random.split(key, 32)
    f32 = jnp.float32
    n_pages = PAST_LEN // PAGE_SIZE
    n_used = DEC_BATCH * n_pages
    n_phys = n_used + max(1, n_used // 4)

    def nrm(k, shape, scale=1.0):
        return jax.random.normal(k, shape, f32) * scale

    x_prompt = nrm(ks[0], (BATCH, SEQ, D_MODEL))
    x_sample = nrm(ks[1], (DEC_BATCH, DEC_SEQ, D_MODEL))
    cache_mla_ckv = nrm(ks[2], (DEPTH, n_phys, PAGE_SIZE, KV_LORA))
    cache_mla_krope = nrm(ks[3], (DEPTH, n_phys, PAGE_SIZE, MLA_ROPE))
    cache_fox_k = nrm(ks[4], (DEPTH, n_phys, PAGE_SIZE, FOX_HEADS, FOX_DIM))
    cache_fox_v = nrm(ks[5], (DEPTH, n_phys, PAGE_SIZE, FOX_HEADS, FOX_DIM))
    cache_fox_logf = jax.nn.log_sigmoid(4.0 + nrm(ks[6], (DEPTH, n_phys, PAGE_SIZE, FOX_HEADS)))
    page_table = jax.random.permutation(ks[7], n_phys)[:n_used].reshape(DEC_BATCH, n_pages).astype(jnp.int32)
    c_prompt = nrm(ks[8], (BATCH, D_MODEL))
    c_sample = nrm(ks[9], (DEC_BATCH, D_MODEL))

    gate_pattern = jnp.concatenate([jnp.zeros((2 * D_MODEL,), f32), jnp.ones((D_MODEL,), f32),
                                    jnp.zeros((2 * D_MODEL,), f32), jnp.ones((D_MODEL,), f32)])
    w_ada = nrm(ks[10], (DEPTH, D_MODEL, 6 * D_MODEL), 0.1 * D_MODEL ** -0.5)
    b_ada = gate_pattern[None, :] + nrm(ks[11], (DEPTH, 6 * D_MODEL), 0.02)
    g_attn = 1.0 + nrm(ks[12], (DEPTH, D_MODEL), 0.02)
    g_mlp = 1.0 + nrm(ks[13], (DEPTH, D_MODEL), 0.02)
    w_in = nrm(ks[14], (DEPTH, D_MODEL, N_IN), D_MODEL ** -0.5)
    b_f = 3.0 + 2.0 * jax.random.uniform(ks[15], (DEPTH, FOX_HEADS), f32)
    g_qa = 1.0 + nrm(ks[16], (DEPTH, Q_LORA), 0.02)
    w_qb = nrm(ks[17], (DEPTH, Q_LORA, MLA_HEADS, MLA_NOPE + MLA_ROPE), Q_LORA ** -0.5)
    g_qn = 1.0 + nrm(ks[18], (DEPTH, MLA_HEADS, MLA_NOPE), 0.02)
    g_qr = 1.0 + nrm(ks[19], (DEPTH, MLA_HEADS, MLA_ROPE), 0.02)
    g_kva = 1.0 + nrm(ks[20], (DEPTH, KV_LORA), 0.02)
    g_kr = 1.0 + nrm(ks[21], (DEPTH, MLA_ROPE), 0.02)
    w_kvb = nrm(ks[22], (DEPTH, KV_LORA, MLA_HEADS, MLA_NOPE + MLA_V), KV_LORA ** -0.5)
    g_kn = 1.0 + nrm(ks[23], (DEPTH, MLA_HEADS, MLA_NOPE), 0.02)
    g_fq = 1.0 + nrm(ks[24], (DEPTH, FOX_HEADS, FOX_DIM), 0.02)
    g_fk = 1.0 + nrm(ks[25], (DEPTH, FOX_HEADS, FOX_DIM), 0.02)
    w_o = nrm(ks[26], (DEPTH, MLA_HEADS * MLA_V + FOX_W, D_MODEL), MIX_WIDTH ** -0.5)
    w_mlp1 = nrm(ks[27], (DEPTH, D_MODEL, D_FF), D_MODEL ** -0.5)
    w_mlp2 = nrm(ks[28], (DEPTH, D_FF, D_MODEL), D_FF ** -0.5)
    return {"x_prompt": x_prompt, "x_sample": x_sample,
            "cache_mla_ckv": cache_mla_ckv, "cache_mla_krope": cache_mla_krope,
            "cache_fox_k": cache_fox_k, "cache_fox_v": cache_fox_v, "cache_fox_logf": cache_fox_logf,
            "page_table": page_table, "c_prompt": c_prompt, "c_sample": c_sample,
            "w_ada": w_ada, "b_ada": b_ada, "g_attn": g_attn, "g_mlp": g_mlp, "w_in": w_in,
            "b_f": b_f, "g_qa": g_qa, "w_qb": w_qb, "g_qn": g_qn, "g_qr": g_qr, "g_kva": g_kva,
            "g_kr": g_kr, "w_kvb": w_kvb, "g_kn": g_kn, "g_fq": g_fq, "g_fk": g_fk, "w_o": w_o,
            "w_mlp1": w_mlp1, "w_mlp2": w_mlp2}


def reference(x_prompt, x_sample, cache_mla_ckv, cache_mla_krope, cache_fox_k, cache_fox_v,
              cache_fox_logf, page_table, c_prompt, c_sample, w_ada, b_ada, g_attn, g_mlp, w_in,
              b_f, g_qa, w_qb, g_qn, g_qr, g_kva, g_kr, w_kvb, g_kn, g_fq, g_fk, w_o, w_mlp1, w_mlp2):
    seq = x_prompt.shape[1]
    dec_seq = x_sample.shape[1]
    past_len = page_table.shape[1] * PAGE_SIZE
    pos_p = jnp.arange(seq, dtype=jnp.int32)
    q_pos_s = past_len + jnp.arange(dec_seq, dtype=jnp.int32)
    k_pos_s = jnp.arange(past_len + dec_seq, dtype=jnp.int32)
    pools = (cache_mla_ckv, cache_mla_krope, cache_fox_k, cache_fox_v, cache_fox_logf)
    weights = (w_ada, b_ada, g_attn, g_mlp, w_in, b_f, g_qa, w_qb, g_qn, g_qr, g_kva, g_kr,
               w_kvb, g_kn, g_fq, g_fk, w_o, w_mlp1, w_mlp2)
    y_prompt, y_sample = x_prompt, x_sample
    rows_p, rows_s = [], []
    for l in range(DEPTH):
        w_l = tuple(p[l] for p in weights)
        past = tuple(gather_pages(pool[l], page_table) for pool in pools)
        y_prompt, new_p = layer_step(y_prompt, c_prompt, pos_p, pos_p, None, w_l)
        y_sample, new_s = layer_step(y_sample, c_sample, q_pos_s, k_pos_s, past, w_l)
        rows_p.append(new_p)
        rows_s.append(new_s)
    p_ckv, p_krope, p_k, p_v, p_logf = [jnp.stack(t, axis=0) for t in zip(*rows_p)]
    s_ckv, s_krope, s_k, s_v, s_logf = [jnp.stack(t, axis=0) for t in zip(*rows_s)]
    return (y_prompt, y_sample, p_ckv, p_krope, p_k, p_v, p_logf, s_ckv, s_krope, s_k, s_v, s_logf)
```

```python
import functools
import math

import jax
import jax.numpy as jnp
import numpy as np
from jax import lax
from jax.experimental import pallas as pl
from jax.experimental.pallas import tpu as pltpu

F32 = jnp.float32
BF16 = jnp.bfloat16

LANES = 128
SUBLANES = 8
VMEM_LIMIT = 56 * 1024 * 1024

MLA_NOPE = 64
MLA_ROPE = 32
MLA_V = 64
FOX_DIM = 64
PAGE = 128
ROPE_THETA = 10000.0
EPS = 1e-6
NEG_INF = -1e30


def _dot(a, b):
    return jnp.dot(a.astype(BF16), b.astype(BF16), preferred_element_type=F32)


def _dot_nt(a, b):
    return lax.dot_general(a.astype(BF16), b.astype(BF16), (((1,), (1,)), ((), ())),
                           preferred_element_type=F32)


def _split3(x):
    hi = x.astype(BF16)
    r1 = x - hi.astype(F32)
    mid = r1.astype(BF16)
    lo = (r1 - mid.astype(F32)).astype(BF16)
    return hi, mid, lo


def _const_spec(shape):
    nd = len(shape)
    return pl.BlockSpec(shape, lambda *_, _nd=nd: (0,) * _nd)


def _group_mean(sq, e_ref):
    n = sq.shape[1] // LANES
    e = e_ref[...]
    parts = [_dot(sq[:, g * LANES:(g + 1) * LANES], e) for g in range(n)]
    return parts[0] if n == 1 else jnp.concatenate(parts, axis=1)


def _log_sigmoid(x):
    return jnp.minimum(x, 0.0) - jnp.log1p(jnp.exp(-jnp.abs(x)))


def _ada_kernel(c_ref, w_ref, b_ref, o_ref):
    c = c_ref[...]
    o_ref[...] = _dot(c * jax.nn.sigmoid(c), w_ref[...]) + b_ref[...]


def _ada_call(c, w_ada, b_ada):
    rows, d = c.shape
    n_chunks = w_ada.shape[1] // d
    return pl.pallas_call(
        _ada_kernel,
        grid=(n_chunks,),
        in_specs=[pl.BlockSpec((rows, d), lambda i: (0, 0)),
                  pl.BlockSpec((d, d), lambda i: (0, i)),
                  pl.BlockSpec((1, d), lambda i: (0, i))],
        out_specs=pl.BlockSpec((rows, d), lambda i: (0, i)),
        out_shape=jax.ShapeDtypeStruct((rows, n_chunks * d), F32),
        compiler_params=pltpu.CompilerParams(dimension_semantics=("arbitrary",)),
        name="adaln_mod",
    )(c, w_ada, b_ada.reshape(1, -1))


def _token_kernel(x_ref, sh_ref, sc_ref, gattn_ref,
                  wq_ref, gqa_ref, wqb_ref, gq_ref, eq_ref, tqc_ref, tqs1_ref, tqs2_ref,
                  wkv_ref, gkva_ref, wkn_ref, gk_ref, ek_ref, wv_ref,
                  wmisc_ref, gkr_ref, tkc_ref, tks1_ref, tks2_ref, rep_ref,
                  wf_ref, gfq_ref, gfk_ref, ef_ref, bf_ref,
                  *out_refs, prompt):
    if prompt:
        (ckv_o, krope_o, fk_o, fv_o, logf_o, q_o, fq_o, kmla_o, vmla_o, fkb_o, fvb_o) = out_refs
    else:
        (ckv_o, krope_o, fk_o, fv_o, logf_o, q_o, fq_o) = out_refs
    sb, tb, d = x_ref.shape
    tm = sb * tb

    def to3(a):
        return a.reshape(sb, tb, a.shape[-1])

    def tiled(t_ref, width):
        t = t_ref[...]
        reps = width // t.shape[-1]
        return t if reps == 1 else jnp.concatenate([t] * reps, axis=-1)

    x = x_ref[...]
    ms = jnp.mean(x * x, axis=-1, keepdims=True)
    h = x * lax.rsqrt(ms + EPS) * gattn_ref[...]
    h = h * (1.0 + sc_ref[...]) + sh_ref[...]
    hb = h.reshape(tm, d).astype(BF16)

    zq = _dot(hb, wq_ref[...])
    ql = zq * lax.rsqrt(jnp.mean(zq * zq, axis=-1, keepdims=True) + EPS) * gqa_ref[...]
    qm = _dot(ql, wqb_ref[...])
    qn = qm * lax.rsqrt(_group_mean(qm * qm, eq_ref) + EPS) * gq_ref[...]
    wq_lanes = qn.shape[1]
    half = MLA_ROPE // 2
    q_rot = (to3(qn) * tiled(tqc_ref, wq_lanes)
             + to3(pltpu.roll(qn, wq_lanes - half, 1)) * tiled(tqs1_ref, wq_lanes)
             + to3(pltpu.roll(qn, half, 1)) * tiled(tqs2_ref, wq_lanes))
    q_o[...] = q_rot.astype(q_o.dtype)

    zkv = _dot(hb, wkv_ref[...])
    ckv = zkv * lax.rsqrt(jnp.mean(zkv * zkv, axis=-1, keepdims=True) + EPS) * gkva_ref[...]
    ckv_o[...] = to3(ckv)

    zm = _dot(hb, wmisc_ref[...])
    lane = lax.broadcasted_iota(jnp.int32, zm.shape, 1)
    kr = jnp.where(lane < MLA_ROPE, zm, 0.0)
    krn = kr * lax.rsqrt(jnp.sum(kr * kr, axis=-1, keepdims=True) * (1.0 / MLA_ROPE) + EPS) * gkr_ref[...]
    kro = (to3(krn) * tkc_ref[...]
           + to3(pltpu.roll(krn, LANES - half, 1)) * tks1_ref[...]
           + to3(pltpu.roll(krn, half, 1)) * tks2_ref[...])
    krope_o[...] = kro[:, :, :MLA_ROPE]
    logf = _log_sigmoid(zm + bf_ref[...])
    n_fh = logf_o.shape[-1]
    logf_o[...] = to3(pltpu.roll(logf, LANES - MLA_ROPE, 1))[:, :, :n_fh]

    zf = _dot(hb, wf_ref[...])
    fw = zf.shape[1] // 3
    fq_raw, fk_raw, fv = zf[:, :fw], zf[:, fw:2 * fw], zf[:, 2 * fw:]
    fq = fq_raw * lax.rsqrt(_group_mean(fq_raw * fq_raw, ef_ref) + EPS) * gfq_ref[...]
    fk = fk_raw * lax.rsqrt(_group_mean(fk_raw * fk_raw, ef_ref) + EPS) * gfk_ref[...]
    fq_o[...] = to3(fq).astype(fq_o.dtype)
    fk_o[...] = to3(fk)
    fv_o[...] = to3(fv)

    if prompt:
        ckv_b = ckv.astype(BF16)
        kraw = _dot(ckv_b, wkn_ref[...])
        kn = kraw * lax.rsqrt(_group_mean(kraw * kraw, ek_ref) + EPS) * gk_ref[...]
        kmla = kn + _dot(kro.reshape(tm, LANES), rep_ref[...])
        kmla_o[...] = to3(kmla).astype(kmla_o.dtype)
        vmla_o[...] = to3(_dot(ckv_b, wv_ref[...])).astype(vmla_o.dtype)
        fkb_o[...] = to3(fk).astype(fkb_o.dtype)
        fvb_o[...] = to3(fv).astype(fvb_o.dtype)


def _token_call(x, shift, scale, consts, tables, *, prompt, sb, tb):
    nseq, slen, d = x.shape
    grid = (nseq // sb, slen // tb)
    (g_attn, wq, gqa, wqb, gq, eq, wkv, gkva, wkn, gk, ek, wv, wmisc, gkr, rep,
     wf, gfq, gfk, ef, bfv) = consts
    tqc, tqs1, tqs2, tkc, tks1, tks2 = tables

    def tok(width):
        return pl.BlockSpec((sb, tb, width), lambda i, j: (i, j, 0))

    mod_spec = pl.BlockSpec((sb, 1, d), lambda i, j: (i, 0, 0))
    tab_spec = pl.BlockSpec((1, tb, LANES), lambda i, j: (0, j, 0))
    cs = _const_spec
    in_specs = [tok(d), mod_spec, mod_spec, cs(g_attn.shape),
                cs(wq.shape), cs(gqa.shape), cs(wqb.shape), cs(gq.shape), cs(eq.shape),
                tab_spec, tab_spec, tab_spec,
                cs(wkv.shape), cs(gkva.shape), cs(wkn.shape), cs(gk.shape), cs(ek.shape), cs(wv.shape),
                cs(wmisc.shape), cs(gkr.shape), tab_spec, tab_spec, tab_spec, cs(rep.shape),
                cs(wf.shape), cs(gfq.shape), cs(gfk.shape), cs(ef.shape), cs(bfv.shape)]
    kv_lora = wkv.shape[1]
    fw = wf.shape[1] // 3
    n_fh = fw // FOX_DIM
    qw = wqb.shape[1]

    def sds(width, dtype):
        return jax.ShapeDtypeStruct((nseq, slen, width), dtype)

    out_shape = [sds(kv_lora, F32), sds(MLA_ROPE, F32), sds(fw, F32), sds(fw, F32), sds(n_fh, F32),
                 sds(qw, BF16), sds(fw, BF16)]
    out_specs = [tok(kv_lora), tok(MLA_ROPE), tok(fw), tok(fw), tok(n_fh), tok(qw), tok(fw)]
    if prompt:
        out_shape += [sds(qw, BF16), sds(wv.shape[1], BF16), sds(fw, BF16), sds(fw, BF16)]
        out_specs += [tok(qw), tok(wv.shape[1]), tok(fw), tok(fw)]
    return pl.pallas_call(
        functools.partial(_token_kernel, prompt=prompt),
        grid=grid, in_specs=in_specs, out_specs=out_specs, out_shape=out_shape,
        compiler_params=pltpu.CompilerParams(
            dimension_semantics=("parallel", "parallel"), vmem_limit_bytes=VMEM_LIMIT),
        name="token_prompt" if prompt else "token_sample",
    )(x, shift, scale, g_attn, wq, gqa, wqb, gq, eq, tqc, tqs1, tqs2,
      wkv, gkva, wkn, gk, ek, wv, wmisc, gkr, tkc, tks1, tks2, rep, wf, gfq, gfk, ef, bfv)


def _cumsum_prompt_kernel(l_ref, lt_ref, tril_ref, triu_ref, c_ref, ct_ref, carry, carry_t):
    j = pl.program_id(1)

    @pl.when(j == 0)
    def _():
        carry[...] = jnp.zeros_like(carry)
        carry_t[...] = jnp.zeros_like(carry_t)

    tril = tril_ref[...]
    triu = triu_ref[...]
    hi, mid, lo = _split3(l_ref[0])
    c = (jnp.dot(tril, hi, preferred_element_type=F32)
         + jnp.dot(tril, mid, preferred_element_type=F32)
         + jnp.dot(tril, lo, preferred_element_type=F32)) + carry[...]
    c_ref[0] = c
    carry[...] = c[c.shape[0] - 1:, :]
    hi, mid, lo = _split3(lt_ref[0])
    ct = (jnp.dot(hi, triu, preferred_element_type=F32)
          + jnp.dot(mid, triu, preferred_element_type=F32)
          + jnp.dot(lo, triu, preferred_element_type=F32)) + carry_t[...]
    ct_ref[0] = ct
    carry_t[...] = ct[:, ct.shape[1] - 1:]


def _cumsum_prompt_call(logf, logf_t, tile):
    b, s, nh = logf.shape
    tri = np.tril(np.ones((tile, tile), np.float32))
    tril = jnp.asarray(tri, BF16)
    triu = jnp.asarray(tri.T, BF16)
    return pl.pallas_call(
        _cumsum_prompt_kernel,
        grid=(b, s // tile),
        in_specs=[pl.BlockSpec((1, tile, nh), lambda i, j: (i, j, 0)),
                  pl.BlockSpec((1, nh, tile), lambda i, j: (i, 0, j)),
                  _const_spec((tile, tile)), _const_spec((tile, tile))],
        out_specs=[pl.BlockSpec((1, tile, nh), lambda i, j: (i, j, 0)),
                   pl.BlockSpec((1, nh, tile), lambda i, j: (i, 0, j))],
        out_shape=[jax.ShapeDtypeStruct((b, s, nh), F32), jax.ShapeDtypeStruct((b, nh, s), F32)],
        scratch_shapes=[pltpu.VMEM((1, nh), F32), pltpu.VMEM((nh, 1), F32)],
        compiler_params=pltpu.CompilerParams(dimension_semantics=("parallel", "arbitrary")),
        name="cumsum_prompt",
    )(logf, logf_t, tril, triu)


def _cumsum_pages_kernel(lt_ref, triu_ref, o_ref):
    r, nh, p = lt_ref.shape
    triu = triu_ref[...]
    hi, mid, lo = _split3(lt_ref[...].reshape(r * nh, p))
    c = (jnp.dot(hi, triu, preferred_element_type=F32)
         + jnp.dot(mid, triu, preferred_element_type=F32)
         + jnp.dot(lo, triu, preferred_element_type=F32))
    o_ref[...] = c.reshape(r, nh, p)


def _cumsum_pages_call(pool_t, rows):
    n_phys, nh, p = pool_t.shape
    triu = jnp.asarray(np.triu(np.ones((p, p), np.float32)), BF16)
    return pl.pallas_call(
        _cumsum_pages_kernel,
        grid=(n_phys // rows,),
        in_specs=[pl.BlockSpec((rows, nh, p), lambda i: (i, 0, 0)), _const_spec((p, p))],
        out_specs=pl.BlockSpec((rows, nh, p), lambda i: (i, 0, 0)),
        out_shape=jax.ShapeDtypeStruct((n_phys, nh, p), F32),
        compiler_params=pltpu.CompilerParams(dimension_semantics=("parallel",)),
        name="cumsum_pages",
    )(pool_t, triu)


def _flash_kernel(*refs, fox):
    if fox:
        q_ref, k_ref, v_ref, cq_ref, ck_ref, o_ref, m_sc, l_sc, acc_sc = refs
    else:
        q_ref, k_ref, v_ref, o_ref, m_sc, l_sc, acc_sc = refs
    qi = pl.program_id(2)
    ki = pl.program_id(3)
    tq = q_ref.shape[1]
    tk = k_ref.shape[1]
    half = LANES // 2

    @pl.when(ki == 0)
    def _():
        m_sc[...] = jnp.full_like(m_sc, NEG_INF)
        l_sc[...] = jnp.zeros_like(l_sc)
        acc_sc[...] = jnp.zeros_like(acc_sc)

    def step(masked):
        v = v_ref[0]
        if fox:
            qp = q_ref[0]
            kp = k_ref[0]
            lane = lax.broadcasted_iota(jnp.int32, qp.shape, 1)
            qs = (jnp.where(lane < half, qp, jnp.zeros_like(qp)),
                  jnp.where(lane >= half, qp, jnp.zeros_like(qp)))
            ks = (kp, kp)
        else:
            qs = (q_ref[0, :, :LANES], q_ref[0, :, LANES:])
            ks = (k_ref[0, :, :LANES], k_ref[0, :, LANES:])
        if masked:
            row = lax.broadcasted_iota(jnp.int32, (tq, tk), 0)
            col = lax.broadcasted_iota(jnp.int32, (tq, tk), 1)
            keep = col <= row
        for hh in range(2):
            s = _dot_nt(qs[hh], ks[hh])
            if fox:
                s = s + (cq_ref[0, 0, :, hh:hh + 1] - ck_ref[0, 0, hh:hh + 1, :])
            if masked:
                s = jnp.where(keep, s, NEG_INF)
            m_prev = m_sc[hh]
            m_new = jnp.maximum(m_prev, jnp.max(s, axis=1, keepdims=True))
            alpha = jnp.exp(m_prev - m_new)
            p = jnp.exp(s - m_new)
            l_sc[hh] = alpha * l_sc[hh] + jnp.sum(p, axis=1, keepdims=True)
            acc_sc[hh] = alpha * acc_sc[hh] + _dot(p, v)
            m_sc[hh] = m_new

    @pl.when(ki < qi)
    def _():
        step(False)

    @pl.when(ki == qi)
    def _():
        step(True)
        o0 = acc_sc[0] / l_sc[0]
        o1 = acc_sc[1] / l_sc[1]
        lane = lax.broadcasted_iota(jnp.int32, o0.shape, 1)
        o_ref[0] = jnp.where(lane < half, o0, o1).astype(o_ref.dtype)


def _flash_call(q, k, v, cum_q=None, cum_k=None, *, tile, name):
    fox = cum_q is not None
    b, s, vw = v.shape
    hp = vw // LANES
    qw = q.shape[2] // hp
    n_t = s // tile
    grid = (b, hp, n_t, n_t)
    in_specs = [pl.BlockSpec((1, tile, qw), lambda bi, h, qi, ki: (bi, qi, h)),
                pl.BlockSpec((1, tile, qw), lambda bi, h, qi, ki: (bi, jnp.minimum(ki, qi), h)),
                pl.BlockSpec((1, tile, LANES), lambda bi, h, qi, ki: (bi, jnp.minimum(ki, qi), h))]
    args = [q, k, v]
    if fox:
        in_specs += [pl.BlockSpec((1, 1, tile, 2), lambda bi, h, qi, ki: (bi, h, qi, 0)),
                     pl.BlockSpec((1, 1, 2, tile), lambda bi, h, qi, ki: (bi, h, 0, jnp.minimum(ki, qi)))]
        args += [cum_q, cum_k]
    return pl.pallas_call(
        functools.partial(_flash_kernel, fox=fox),
        grid=grid, in_specs=in_specs,
        out_specs=pl.BlockSpec((1, tile, LANES), lambda bi, h, qi, ki: (bi, qi, h)),
        out_shape=jax.ShapeDtypeStruct((b, s, vw), BF16),
        scratch_shapes=[pltpu.VMEM((2, tile, 1), F32), pltpu.VMEM((2, tile, 1), F32),
                        pltpu.VMEM((2, tile, LANES), F32)],
        compiler_params=pltpu.CompilerParams(
            dimension_semantics=("parallel", "parallel", "parallel", "arbitrary"),
            vmem_limit_bytes=VMEM_LIMIT),
        name=name,
    )(*args)


def _head_rows(x, n_heads, width):
    lane = lax.broadcasted_iota(jnp.int32, x.shape, 1)
    return jnp.concatenate(
        [jnp.where((lane >= h * width) & (lane < (h + 1) * width), x, jnp.zeros_like(x))
         for h in range(n_heads)], axis=0)


def _head_diag(full, n_heads, width, t):
    lane = lax.broadcasted_iota(jnp.int32, (t, full.shape[1]), 1)
    out = jnp.zeros((t, full.shape[1]), full.dtype)
    for h in range(n_heads):
        blk = full[h * t:(h + 1) * t, :]
        out = out + jnp.where((lane >= h * width) & (lane < (h + 1) * width), blk, 0.0)
    return out


def _online_update(s, v_b, m_sc, l_sc, acc_sc, first):
    if first:
        m_new = jnp.max(s, axis=1, keepdims=True)
        p = jnp.exp(s - m_new)
        l_sc[...] = jnp.sum(p, axis=1, keepdims=True)
        acc_sc[...] = _dot(p, v_b)
    else:
        m_prev = m_sc[...]
        m_new = jnp.maximum(m_prev, jnp.max(s, axis=1, keepdims=True))
        alpha = jnp.exp(m_prev - m_new)
        p = jnp.exp(s - m_new)
        l_sc[...] = alpha * l_sc[...] + jnp.sum(p, axis=1, keepdims=True)
        acc_sc[...] = alpha * acc_sc[...] + _dot(p, v_b)
    m_sc[...] = m_new


def _new_key_mask(rows, t):
    row = lax.broadcasted_iota(jnp.int32, (rows, PAGE), 0)
    col = lax.broadcasted_iota(jnp.int32, (rows, PAGE), 1)
    return col <= row % t


def _pad_rows(x, rows):
    return jnp.concatenate([x, jnp.zeros((rows - x.shape[0], x.shape[1]), x.dtype)], axis=0)


def _mla_decode_kernel(pt_ref, q_ref, cnew_ref, krnew_ref, sel_ref, gkn_ref, wkn_ref, wv_ref, es_ref,
                       *rest, n_pp, n_heads):
    ckv_refs = rest[:n_pp]
    kr_refs = rest[n_pp:2 * n_pp]
    o_ref, qbd_sc, qr_sc, m_sc, l_sc, acc_sc = rest[2 * n_pp:]
    j = pl.program_id(1)
    t = q_ref.shape[1]

    def scores(ckv_b, kr):
        kraw = _dot(ckv_b, wkn_ref[...])
        s = _dot_nt(qbd_sc[...], kraw)
        ssq = _dot_nt(es_ref[...], kraw * kraw)
        return s * lax.rsqrt(ssq + EPS) + _dot_nt(qr_sc[...], kr)

    @pl.when(j == 0)
    def _():
        qm = q_ref[0].astype(F32)
        qn = _dot(qm, sel_ref[...]) * gkn_ref[...]
        qbd_sc[...] = _head_rows(qn, n_heads, MLA_NOPE).astype(BF16)
        qr_sc[...] = jnp.concatenate(
            [qm[:, h * LANES + MLA_NOPE:h * LANES + MLA_NOPE + MLA_ROPE] for h in range(n_heads)],
            axis=0).astype(BF16)
        ckv_b = _pad_rows(cnew_ref[0], PAGE).astype(BF16)
        s = scores(ckv_b, _pad_rows(krnew_ref[0], PAGE))
        s = jnp.where(_new_key_mask(n_heads * t, t), s, NEG_INF)
        _online_update(s, ckv_b, m_sc, l_sc, acc_sc, first=True)

    ckv_b = jnp.concatenate([r[0].astype(BF16) for r in ckv_refs], axis=0)
    kr = jnp.concatenate([r[0] for r in kr_refs], axis=0)
    _online_update(scores(ckv_b, kr), ckv_b, m_sc, l_sc, acc_sc, first=False)

    @pl.when(j == pl.num_programs(1) - 1)
    def _():
        o_lat = acc_sc[...] / l_sc[...]
        full = _dot(o_lat, wv_ref[...])
        o_ref[0] = _head_diag(full, n_heads, MLA_V, t).astype(o_ref.dtype)


def _page_map(slot, n_pp, n_pages, nd):
    def index_map(b, j, pt):
        logical = n_pages - 1 - (j * n_pp + slot)
        return (pt[b * n_pages + logical],) + (0,) * (nd - 1)
    return index_map


def _mla_decode_call(pt_flat, q, ckv_new, kr_new, sel, gkn, wkn, wv, es, pool_ckv, pool_kr, *, n_pp):
    bd, t, qw = q.shape
    n_pages = pt_flat.shape[0] // bd
    n_heads = qw // LANES
    kv_lora = pool_ckv.shape[-1]

    def batch(shape):
        return pl.BlockSpec((1,) + shape, lambda b, j, pt: (b, 0, 0))

    cs = _const_spec
    in_specs = [batch((t, qw)), batch((t, kv_lora)), batch((t, MLA_ROPE)),
                cs(sel.shape), cs(gkn.shape), cs(wkn.shape), cs(wv.shape), cs(es.shape)]
    in_specs += [pl.BlockSpec((1, PAGE, kv_lora), _page_map(p, n_pp, n_pages, 3)) for p in range(n_pp)]
    in_specs += [pl.BlockSpec((1, PAGE, MLA_ROPE), _page_map(p, n_pp, n_pages, 3)) for p in range(n_pp)]
    ow = wv.shape[1]
    rows = n_heads * t
    grid_spec = pltpu.PrefetchScalarGridSpec(
        num_scalar_prefetch=1, grid=(bd, n_pages // n_pp), in_specs=in_specs,
        out_specs=pl.BlockSpec((1, t, ow), lambda b, j, pt: (b, 0, 0)),
        scratch_shapes=[pltpu.VMEM((rows, wkn.shape[1]), BF16), pltpu.VMEM((rows, MLA_ROPE), BF16),
                        pltpu.VMEM((rows, 1), F32), pltpu.VMEM((rows, 1), F32),
                        pltpu.VMEM((rows, kv_lora), F32)])
    return pl.pallas_call(
        functools.partial(_mla_decode_kernel, n_pp=n_pp, n_heads=n_heads),
        grid_spec=grid_spec,
        out_shape=jax.ShapeDtypeStruct((bd, t, ow), BF16),
        compiler_params=pltpu.CompilerParams(
            dimension_semantics=("parallel", "arbitrary"), vmem_limit_bytes=VMEM_LIMIT),
        name="mla_decode",
    )(pt_flat, q, ckv_new, kr_new, sel, gkn, wkn, wv, es, *([pool_ckv] * n_pp), *([pool_kr] * n_pp))


def _fox_decode_kernel(pt_ref, q_ref, knew_ref, vnew_ref, lnew_ref, lnewt_ref, *rest, n_pp, n_heads):
    k_refs = rest[:n_pp]
    v_refs = rest[n_pp:2 * n_pp]
    c_refs = rest[2 * n_pp:3 * n_pp]
    o_ref, qbd_sc, acol_sc, suf_sc, m_sc, l_sc, acc_sc = rest[3 * n_pp:]
    j = pl.program_id(1)
    t = q_ref.shape[1]

    def head_rows_bias(x):
        return jnp.concatenate(
            [jnp.broadcast_to(x[h:h + 1, :], (t, x.shape[1])) for h in range(n_heads)], axis=0)

    @pl.when(j == 0)
    def _():
        qbd_sc[...] = _head_rows(q_ref[0].astype(F32), n_heads, FOX_DIM).astype(BF16)
        lnew = lnew_ref[0]
        row = lax.broadcasted_iota(jnp.int32, lnew.shape, 0)
        cnew = jnp.zeros_like(lnew)
        for tt in range(t):
            cnew = cnew + jnp.where(row >= tt, lnew[tt:tt + 1, :], 0.0)
        for h in range(n_heads):
            acol_sc[h * t:(h + 1) * t, :] = cnew[:, h:h + 1]
        lnt = lnewt_ref[0]
        lane = lax.broadcasted_iota(jnp.int32, lnt.shape, 1)
        cnew_t = jnp.zeros_like(lnt)
        for tt in range(t):
            cnew_t = cnew_t + jnp.where(lane >= tt, lnt[:, tt:tt + 1], 0.0)
        suf_sc[...] = jnp.zeros_like(suf_sc)
        k_b = _pad_rows(knew_ref[0], PAGE).astype(BF16)
        v_b = _pad_rows(vnew_ref[0], PAGE).astype(BF16)
        s = _dot_nt(qbd_sc[...], k_b) + (acol_sc[...] - head_rows_bias(cnew_t))
        s = jnp.where(_new_key_mask(n_heads * t, t), s, NEG_INF)
        _online_update(s, v_b, m_sc, l_sc, acc_sc, first=True)

    suffix = suf_sc[...]
    biases = []
    for p in range(n_pp):
        c = c_refs[p][0]
        total = c[:, PAGE - 1:PAGE]
        biases.append(suffix + (total - c))
        suffix = suffix + total
    suf_sc[...] = suffix
    bias = biases[0] if n_pp == 1 else jnp.concatenate(biases, axis=1)
    k_b = jnp.concatenate([r[0].astype(BF16) for r in k_refs], axis=0)
    v_b = jnp.concatenate([r[0].astype(BF16) for r in v_refs], axis=0)
    s = _dot_nt(qbd_sc[...], k_b) + (acol_sc[...] + head_rows_bias(bias))
    _online_update(s, v_b, m_sc, l_sc, acc_sc, first=False)

    @pl.when(j == pl.num_programs(1) - 1)
    def _():
        o_ref[0] = _head_diag(acc_sc[...] / l_sc[...], n_heads, FOX_DIM, t).astype(o_ref.dtype)


def _fox_decode_call(pt_flat, q, k_new, v_new, l_new, l_new_t, pool_k, pool_v, pool_c, *, n_pp):
    bd, t, fw = q.shape
    n_pages = pt_flat.shape[0] // bd
    n_heads = fw // FOX_DIM

    def batch(shape):
        return pl.BlockSpec((1,) + shape, lambda b, j, pt: (b, 0, 0))

    in_specs = [batch((t, fw)), batch((t, fw)), batch((t, fw)), batch((t, n_heads)), batch((n_heads, PAGE))]
    in_specs += [pl.BlockSpec((1, PAGE, fw), _page_map(p, n_pp, n_pages, 3)) for p in range(n_pp)]
    in_specs += [pl.BlockSpec((1, PAGE, fw), _page_map(p, n_pp, n_pages, 3)) for p in range(n_pp)]
    in_specs += [pl.BlockSpec((1, n_heads, PAGE), _page_map(p, n_pp, n_pages, 3)) for p in range(n_pp)]
    rows = n_heads * t
    grid_spec = pltpu.PrefetchScalarGridSpec(
        num_scalar_prefetch=1, grid=(bd, n_pages // n_pp), in_specs=in_specs,
        out_specs=pl.BlockSpec((1, t, fw), lambda b, j, pt: (b, 0, 0)),
        scratch_shapes=[pltpu.VMEM((rows, fw), BF16), pltpu.VMEM((rows, 1), F32),
                        pltpu.VMEM((n_heads, 1), F32),
                        pltpu.VMEM((rows, 1), F32), pltpu.VMEM((rows, 1), F32),
                        pltpu.VMEM((rows, fw), F32)])
    return pl.pallas_call(
        functools.partial(_fox_decode_kernel, n_pp=n_pp, n_heads=n_heads),
        grid_spec=grid_spec,
        out_shape=jax.ShapeDtypeStruct((bd, t, fw), BF16),
        compiler_params=pltpu.CompilerParams(
            dimension_semantics=("parallel", "arbitrary"), vmem_limit_bytes=VMEM_LIMIT),
        name="fox_decode",
    )(pt_flat, q, k_new, v_new, l_new, l_new_t,
      *([pool_k] * n_pp), *([pool_v] * n_pp), *([pool_c] * n_pp))


def _out_mlp_kernel(x_ref, om_ref, of_ref, ga_ref, shm_ref, scm_ref, gm_ref, gmlp_ref,
                    wom_ref, wof_ref, w1_ref, w2_ref, y_ref, *, ff_chunk):
    sb, tb, d = x_ref.shape
    tm = sb * tb
    o = (_dot(om_ref[...].reshape(tm, om_ref.shape[-1]), wom_ref[...])
         + _dot(of_ref[...].reshape(tm, of_ref.shape[-1]), wof_ref[...]))
    x1 = x_ref[...] + ga_ref[...] * o.reshape(sb, tb, d)
    ms = jnp.mean(x1 * x1, axis=-1, keepdims=True)
    h = x1 * lax.rsqrt(ms + EPS) * gmlp_ref[...]
    h = h * (1.0 + scm_ref[...]) + shm_ref[...]
    hb = h.reshape(tm, d).astype(BF16)
    d_ff = w1_ref.shape[1]
    acc = jnp.zeros((tm, d), F32)
    for c in range(d_ff // ff_chunk):
        u = jnp.maximum(_dot(hb, w1_ref[:, c * ff_chunk:(c + 1) * ff_chunk]), 0.0)
        acc = acc + _dot(u * u, w2_ref[c * ff_chunk:(c + 1) * ff_chunk, :])
    y_ref[...] = x1 + gm_ref[...] * acc.reshape(sb, tb, d)


def _out_mlp_call(x, o_mla, o_fox, gate_a, shift_m, scale_m, gate_m, g_mlp, wom, wof, w1, w2, *, sb, tb):
    nseq, slen, d = x.shape

    def tok(width):
        return pl.BlockSpec((sb, tb, width), lambda i, j: (i, j, 0))

    mod_spec = pl.BlockSpec((sb, 1, d), lambda i, j: (i, 0, 0))

    def weight(shape):
        return pl.BlockSpec(shape, lambda i, j: (0, 0), pipeline_mode=pl.Buffered(1))

    return pl.pallas_call(
        functools.partial(_out_mlp_kernel, ff_chunk=min(1024, w1.shape[1])),
        grid=(nseq // sb, slen // tb),
        in_specs=[tok(d), tok(o_mla.shape[-1]), tok(o_fox.shape[-1]),
                  mod_spec, mod_spec, mod_spec, mod_spec, _const_spec(g_mlp.shape),
                  weight(wom.shape), weight(wof.shape), weight(w1.shape), weight(w2.shape)],
        out_specs=tok(d),
        out_shape=jax.ShapeDtypeStruct((nseq, slen, d), F32),
        compiler_params=pltpu.CompilerParams(
            dimension_semantics=("parallel", "parallel"), vmem_limit_bytes=VMEM_LIMIT),
        name="out_mlp",
    )(x, o_mla, o_fox, gate_a, shift_m, scale_m, gate_m, g_mlp, wom, wof, w1, w2)


def _rope_tables(pos, scale_q):
    half = MLA_ROPE // 2
    inv_freq = ROPE_THETA ** (-jnp.arange(half, dtype=F32) / half)
    ang = pos.astype(F32)[:, None] * inv_freq[None, :]
    cos, sin = jnp.cos(ang), jnp.sin(ang)
    n = pos.shape[0]
    z = lambda w: jnp.zeros((n, w), F32)
    one = jnp.ones((n, MLA_NOPE), F32)
    tqc = jnp.concatenate([one, cos, cos, z(LANES - MLA_NOPE - MLA_ROPE)], axis=1) * scale_q
    tqs1 = jnp.concatenate([z(MLA_NOPE), -sin, z(LANES - MLA_NOPE - half)], axis=1) * scale_q
    tqs2 = jnp.concatenate([z(MLA_NOPE + half), sin, z(LANES - MLA_NOPE - MLA_ROPE)], axis=1) * scale_q
    tkc = jnp.concatenate([cos, cos, z(LANES - MLA_ROPE)], axis=1)
    tks1 = jnp.concatenate([-sin, z(LANES - half)], axis=1)
    tks2 = jnp.concatenate([z(half), sin, z(LANES - MLA_ROPE)], axis=1)
    return tuple(t[None] for t in (tqc, tqs1, tqs2, tkc, tks1, tks2))


def _layer_consts(w_in, b_f, g_attn, g_qa, w_qb, g_qn, g_qr, g_kva, g_kr, w_kvb, g_kn, g_fq, g_fk):
    d = w_in.shape[0]
    q_lora, n_mh, _ = w_qb.shape
    kv_lora = w_kvb.shape[0]
    n_fh = b_f.shape[0]
    fw = n_fh * FOX_DIM
    c0, c1, c2 = q_lora, q_lora + kv_lora, q_lora + kv_lora + MLA_ROPE
    c3 = c2 + 3 * fw
    pad_h = LANES - MLA_NOPE - MLA_ROPE
    row = lambda v: v.reshape(1, -1).astype(F32)

    wq = w_in[:, :c0].astype(BF16)
    wkv = w_in[:, c0:c1].astype(BF16)
    wmisc = jnp.concatenate([w_in[:, c1:c2], w_in[:, c3:c3 + n_fh],
                             jnp.zeros((d, LANES - MLA_ROPE - n_fh), F32)], axis=1).astype(BF16)
    wf = w_in[:, c2:c3].astype(BF16)
    wqb = jnp.pad(w_qb, ((0, 0), (0, 0), (0, pad_h))).reshape(q_lora, n_mh * LANES).astype(BF16)
    gq = row(jnp.concatenate([g_qn, g_qr, jnp.zeros((n_mh, pad_h), F32)], axis=1))
    wkn = jnp.pad(w_kvb[:, :, :MLA_NOPE], ((0, 0), (0, 0), (0, LANES - MLA_NOPE)))
    wkn = wkn.reshape(kv_lora, n_mh * LANES).astype(BF16)
    gk = row(jnp.pad(g_kn, ((0, 0), (0, LANES - MLA_NOPE))))
    wkn_c = w_kvb[:, :, :MLA_NOPE].reshape(kv_lora, n_mh * MLA_NOPE).astype(BF16)
    wv_c = w_kvb[:, :, MLA_NOPE:].reshape(kv_lora, n_mh * MLA_V).astype(BF16)
    gkr = row(jnp.pad(g_kr, (0, LANES - MLA_ROPE)))
    bfv = row(jnp.concatenate([jnp.zeros((MLA_ROPE,), F32), b_f, jnp.zeros((LANES - MLA_ROPE - n_fh,), F32)]))
    gfq = row(g_fq) * (FOX_DIM ** -0.5)
    gfk = row(g_fk)

    idx = np.arange(LANES)
    nope = idx < MLA_NOPE
    rope = (idx >= MLA_NOPE) & (idx < MLA_NOPE + MLA_ROPE)
    eq = (np.outer(nope, nope) / MLA_NOPE + np.outer(rope, rope) / MLA_ROPE).astype(np.float32)
    ek = (np.outer(nope, nope) / MLA_NOPE).astype(np.float32)
    ef = ((idx[:, None] // FOX_DIM) == (idx[None, :] // FOX_DIM)).astype(np.float32) / FOX_DIM
    rep = np.zeros((LANES, n_mh * LANES), np.float32)
    for h in range(n_mh):
        rep[np.arange(MLA_ROPE), h * LANES + MLA_NOPE + np.arange(MLA_ROPE)] = 1.0
    sel = np.zeros((n_mh * LANES, n_mh * MLA_NOPE), np.float32)
    for h in range(n_mh):
        sel[h * LANES + np.arange(MLA_NOPE), h * MLA_NOPE + np.arange(MLA_NOPE)] = 1.0
    t_dec = SUBLANES
    es = np.zeros((n_mh * t_dec, n_mh * MLA_NOPE), np.float32)
    for h in range(n_mh):
        es[h * t_dec:(h + 1) * t_dec, h * MLA_NOPE:(h + 1) * MLA_NOPE] = 1.0 / MLA_NOPE

    token_consts = (row(g_attn), wq, row(g_qa), wqb, gq, jnp.asarray(eq, BF16),
                    wkv, row(g_kva), wkn, gk, jnp.asarray(ek, BF16), wv_c,
                    wmisc, gkr, jnp.asarray(rep, BF16), wf, gfq, gfk, jnp.asarray(ef, BF16), bfv)
    decode_consts = (jnp.asarray(sel, BF16), row(g_kn), wkn_c, wv_c, jnp.asarray(es, BF16))
    return token_consts, decode_consts


def kernel(x_prompt, x_sample, cache_mla_ckv, cache_mla_krope, cache_fox_k, cache_fox_v, cache_fox_logf,
           page_table, c_prompt, c_sample, w_ada, b_ada, g_attn, g_mlp, w_in, b_f, g_qa, w_qb, g_qn,
           g_qr, g_kva, g_kr, w_kvb, g_kn, g_fq, g_fk, w_o, w_mlp1, w_mlp2):
    depth = w_ada.shape[0]
    b, s, d = x_prompt.shape
    bd, t_dec, _ = x_sample.shape
    assert t_dec == SUBLANES, "decode kernels lay the new tokens on one sublane tile"
    n_pages = page_table.shape[1]
    past_len = n_pages * PAGE
    n_phys = cache_mla_ckv.shape[1]
    n_fh = b_f.shape[1]
    fw = n_fh * FOX_DIM
    n_mh = w_qb.shape[2]
    assert n_fh % 2 == 0 and n_mh % 2 == 0

    tile_p = min(512, s)
    sb_s = min(64, bd)
    n_pp = min(8, n_pages)
    cum_rows = math.gcd(n_phys, 256)
    assert s % tile_p == 0 and bd % sb_s == 0 and n_pages % n_pp == 0

    scale_mla = (MLA_NOPE + MLA_ROPE) ** -0.5
    tabs_p = _rope_tables(jnp.arange(s, dtype=jnp.int32), scale_mla)
    tabs_s = _rope_tables(past_len + jnp.arange(t_dec, dtype=jnp.int32), scale_mla)
    pt_flat = page_table.reshape(-1).astype(jnp.int32)

    y_p, y_s = x_prompt, x_sample
    rows_p, rows_s = [], []
    for l in range(depth):
        token_consts, decode_consts = _layer_consts(
            w_in[l], b_f[l], g_attn[l], g_qa[l], w_qb[l], g_qn[l], g_qr[l], g_kva[l], g_kr[l],
            w_kvb[l], g_kn[l], g_fq[l], g_fk[l])
        mod = _ada_call(jnp.concatenate([c_prompt, c_sample], axis=0), w_ada[l], b_ada[l])
        mods_p = [m[:, None, :] for m in jnp.split(mod[:b], 6, axis=-1)]
        mods_s = [m[:, None, :] for m in jnp.split(mod[b:], 6, axis=-1)]

        (p_ckv, p_krope, p_k, p_v, p_logf, q_mla, fq_b, k_mla, v_mla, fk_b, fv_b) = _token_call(
            y_p, mods_p[0], mods_p[1], token_consts, tabs_p, prompt=True, sb=1, tb=tile_p)
        (s_ckv, s_krope, s_k, s_v, s_logf, q_mla_s, fq_s) = _token_call(
            y_s, mods_s[0], mods_s[1], token_consts, tabs_s, prompt=False, sb=sb_s, tb=t_dec)

        cum, cum_t = _cumsum_prompt_call(p_logf, jnp.swapaxes(p_logf, 1, 2), tile_p)
        cum_q = jnp.transpose(cum.reshape(b, s, n_fh // 2, 2), (0, 2, 1, 3))
        cum_k = cum_t.reshape(b, n_fh // 2, 2, s)
        o_mla_p = _flash_call(q_mla, k_mla, v_mla, tile=tile_p, name="mla_prompt")
        o_fox_p = _flash_call(fq_b, fk_b, fv_b, cum_q, cum_k, tile=tile_p, name="fox_prompt")

        sel, gkn, wkn_c, wv_c, es = decode_consts
        o_mla_s = _mla_decode_call(pt_flat, q_mla_s, s_ckv, s_krope, sel, gkn, wkn_c, wv_c, es,
                                   cache_mla_ckv[l], cache_mla_krope[l], n_pp=n_pp)
        pool_c = _cumsum_pages_call(jnp.swapaxes(cache_fox_logf[l], 1, 2), cum_rows)
        l_new_t = jnp.pad(jnp.swapaxes(s_logf, 1, 2), ((0, 0), (0, 0), (0, PAGE - t_dec)))
        o_fox_s = _fox_decode_call(pt_flat, fq_s, s_k, s_v, s_logf, l_new_t,
                                   cache_fox_k[l].reshape(n_phys, PAGE, fw),
                                   cache_fox_v[l].reshape(n_phys, PAGE, fw), pool_c, n_pp=n_pp)

        wo = w_o[l].astype(BF16)
        wom, wof = wo[:n_mh * MLA_V], wo[n_mh * MLA_V:]
        w1 = w_mlp1[l].astype(BF16)
        w2 = w_mlp2[l].astype(BF16)
        g_mlp_row = g_mlp[l].reshape(1, -1)
        y_p = _out_mlp_call(y_p, o_mla_p, o_fox_p, mods_p[2], mods_p[3], mods_p[4], mods_p[5],
                            g_mlp_row, wom, wof, w1, w2, sb=1, tb=tile_p)
        y_s = _out_mlp_call(y_s, o_mla_s, o_fox_s, mods_s[2], mods_s[3], mods_s[4], mods_s[5],
                            g_mlp_row, wom, wof, w1, w2, sb=sb_s, tb=t_dec)

        rows_p.append((p_ckv, p_krope, p_k.reshape(b, s, n_fh, FOX_DIM), p_v.reshape(b, s, n_fh, FOX_DIM), p_logf))
        rows_s.append((s_ckv, s_krope, s_k.reshape(bd, t_dec, n_fh, FOX_DIM),
                       s_v.reshape(bd, t_dec, n_fh, FOX_DIM), s_logf))

    outs_p = [jnp.stack(t, axis=0) for t in zip(*rows_p)]
    outs_s = [jnp.stack(t, axis=0) for t in zip(*rows_s)]
    return (y_p, y_s, *outs_p, *outs_s)
```

```python
import functools
import math

import jax
import jax.numpy as jnp
import numpy as np
from jax import lax
from jax.experimental import pallas as pl
from jax.experimental.pallas import tpu as pltpu

F32 = jnp.float32
BF16 = jnp.bfloat16

LANES = 128
SUBLANES = 8
BF16_ROWS = 16
VMEM_LIMIT = 56 * 1024 * 1024

MLA_NOPE = 64
MLA_ROPE = 32
MLA_V = 64
FOX_DIM = 64
PAGE = 128
ROPE_THETA = 10000.0
EPS = 1e-6
NEG_INF = -1e30
LOG2E = math.log2(math.e)

V_ROWS = MLA_V + BF16_ROWS
FLASH_CQ = 512
FLASH_CK = 512
BIAS_LANE = FOX_DIM
LOGF_LANE = MLA_ROPE


def _dot(a, b):
    return jnp.dot(a.astype(BF16), b.astype(BF16), preferred_element_type=F32)


def _dot_nt(a, b):
    return lax.dot_general(a.astype(BF16), b.astype(BF16), (((1,), (1,)), ((), ())),
                           preferred_element_type=F32)


def _split3(x):
    hi = x.astype(BF16)
    r1 = x - hi.astype(F32)
    mid = r1.astype(BF16)
    lo = (r1 - mid.astype(F32)).astype(BF16)
    return hi, mid, lo


def _const_spec(shape):
    nd = len(shape)
    return pl.BlockSpec(shape, lambda *_, _nd=nd: (0,) * _nd)


def _group_mean(sq, e_ref):
    n = sq.shape[1] // LANES
    e = e_ref[...]
    parts = [_dot(sq[:, g * LANES:(g + 1) * LANES], e) for g in range(n)]
    return parts[0] if n == 1 else jnp.concatenate(parts, axis=1)


def _log_sigmoid(x):
    return jnp.minimum(x, 0.0) - jnp.log1p(jnp.exp(-jnp.abs(x)))


def _ada_kernel(c_ref, w_ref, b_ref, o_ref):
    c = c_ref[...]
    o_ref[...] = _dot(c * jax.nn.sigmoid(c), w_ref[...]) + b_ref[...]


def _ada_call(c, w_ada, b_ada):
    rows, d = c.shape
    n_chunks = w_ada.shape[1] // d
    return pl.pallas_call(
        _ada_kernel,
        grid=(n_chunks,),
        in_specs=[pl.BlockSpec((rows, d), lambda i: (0, 0)),
                  pl.BlockSpec((d, d), lambda i: (0, i)),
                  pl.BlockSpec((1, d), lambda i: (0, i))],
        out_specs=pl.BlockSpec((rows, d), lambda i: (0, i)),
        out_shape=jax.ShapeDtypeStruct((rows, n_chunks * d), F32),
        compiler_params=pltpu.CompilerParams(dimension_semantics=("arbitrary",)),
        name="adaln_mod",
    )(c, w_ada, b_ada.reshape(1, -1))


def _token_kernel(x_ref, sh_ref, sc_ref, gattn_ref,
                  wq_ref, gqa_ref, wqb_ref, gq_ref, eq_ref, tqc_ref, tqs1_ref, tqs2_ref,
                  wkv_ref, gkva_ref, wmisc_ref, gkr_ref, tkc_ref, tks1_ref, tks2_ref,
                  wf_ref, gfq_ref, gfk_ref, ef_ref, bf_ref,
                  *rest, prompt):
    if prompt:
        (wkn_ref, gk_ref, ek_ref, rep_ref, wvt_ref, wfvt_ref, ones_ref, placef_ref,
         tril_ref, placeq_ref, placek_ref, constq_ref, constk_ref,
         ckv_o, krope_o, fk_o, fv_o, logf_o, q_o,
         kmla_o, vtm_o, fqa_o, fka_o, vtf_o, carry_sc) = rest
    else:
        (ckv_o, krope_o, fk_o, fv_o, logf_o, q_o, fq_o) = rest
    sb, tb, d = x_ref.shape
    tm = sb * tb

    def to3(a):
        return a.reshape(sb, tb, a.shape[-1])

    def tiled(t_ref, width):
        t = t_ref[...]
        reps = width // t.shape[-1]
        return t if reps == 1 else jnp.concatenate([t] * reps, axis=-1)

    x = x_ref[...]
    ms = jnp.mean(x * x, axis=-1, keepdims=True)
    h = x * lax.rsqrt(ms + EPS) * gattn_ref[...]
    h = h * (1.0 + sc_ref[...]) + sh_ref[...]
    hb = h.reshape(tm, d).astype(BF16)

    zq = _dot(hb, wq_ref[...])
    ql = zq * lax.rsqrt(jnp.mean(zq * zq, axis=-1, keepdims=True) + EPS) * gqa_ref[...]
    qm = _dot(ql, wqb_ref[...])
    qn = qm * lax.rsqrt(_group_mean(qm * qm, eq_ref) + EPS) * gq_ref[...]
    wq_lanes = qn.shape[1]
    half = MLA_ROPE // 2
    q_rot = (to3(qn) * tiled(tqc_ref, wq_lanes)
             + to3(pltpu.roll(qn, wq_lanes - half, 1)) * tiled(tqs1_ref, wq_lanes)
             + to3(pltpu.roll(qn, half, 1)) * tiled(tqs2_ref, wq_lanes))
    q_o[...] = q_rot.astype(q_o.dtype)

    zkv = _dot(hb, wkv_ref[...])
    ckv = zkv * lax.rsqrt(jnp.mean(zkv * zkv, axis=-1, keepdims=True) + EPS) * gkva_ref[...]
    ckv_o[...] = to3(ckv)

    zm = _dot(hb, wmisc_ref[...])
    lane = lax.broadcasted_iota(jnp.int32, zm.shape, 1)
    kr = jnp.where(lane < MLA_ROPE, zm, 0.0)
    krn = kr * lax.rsqrt(jnp.sum(kr * kr, axis=-1, keepdims=True) * (1.0 / MLA_ROPE) + EPS) * gkr_ref[...]
    kro = (to3(krn) * tkc_ref[...]
           + to3(pltpu.roll(krn, LANES - half, 1)) * tks1_ref[...]
           + to3(pltpu.roll(krn, half, 1)) * tks2_ref[...])
    krope_o[...] = kro[:, :, :MLA_ROPE]
    logf = _log_sigmoid(zm + bf_ref[...])
    n_fh = logf_o.shape[-1]
    logf_o[...] = to3(pltpu.roll(logf, LANES - LOGF_LANE, 1))[:, :, :n_fh]

    zf = _dot(hb, wf_ref[...])
    fw = zf.shape[1] // 3
    fq_raw, fk_raw, fv = zf[:, :fw], zf[:, fw:2 * fw], zf[:, 2 * fw:]
    fq = fq_raw * lax.rsqrt(_group_mean(fq_raw * fq_raw, ef_ref) + EPS) * gfq_ref[...]
    fk = fk_raw * lax.rsqrt(_group_mean(fk_raw * fk_raw, ef_ref) + EPS) * gfk_ref[...]
    fk_o[...] = to3(fk)
    fv_o[...] = to3(fv)

    if not prompt:
        fq_o[...] = to3(fq).astype(fq_o.dtype)
        return

    ckv_b = ckv.astype(BF16)
    kraw = _dot(ckv_b, wkn_ref[...])
    kn = kraw * lax.rsqrt(_group_mean(kraw * kraw, ek_ref) + EPS) * gk_ref[...]
    kmla = kn + _dot(kro.reshape(tm, LANES), rep_ref[...])
    kmla_o[0] = kmla.astype(kmla_o.dtype)
    ones_col = ones_ref[...]
    vtm_o[0] = (_dot_nt(wvt_ref[...], ckv_b) + ones_col).astype(vtm_o.dtype)
    vtf_o[0] = (_dot_nt(wfvt_ref[...], hb) + ones_col).astype(vtf_o.dtype)

    @pl.when(pl.program_id(1) == 0)
    def _():
        carry_sc[...] = jnp.zeros_like(carry_sc)

    lw = jnp.where((lane >= LOGF_LANE) & (lane < LOGF_LANE + n_fh), logf, 0.0)
    tril = tril_ref[...]
    hi, mid, lo = _split3(lw)
    cum = (jnp.dot(tril, hi, preferred_element_type=F32) + jnp.dot(tril, mid, preferred_element_type=F32)
           + jnp.dot(tril, lo, preferred_element_type=F32)) + carry_sc[...]
    carry_sc[...] = cum[tm - 1:, :]
    hi, mid, lo = _split3(cum * LOG2E)
    parts = (hi.astype(F32) + pltpu.roll(mid.astype(F32), n_fh, 1)
             + pltpu.roll(lo.astype(F32), 2 * n_fh, 1)).astype(BF16)
    fqa = _dot(fq, placef_ref[...]) + _dot(parts, placeq_ref[...]) + constq_ref[...]
    fka = _dot(fk, placef_ref[...]) + _dot(parts, placek_ref[...]) + constk_ref[...]
    fqa_o[0] = fqa.astype(fqa_o.dtype)
    fka_o[0] = fka.astype(fka_o.dtype)


def _token_call(x, shift, scale, consts, prompt_consts, tables, *, prompt, sb, tb):
    nseq, slen, d = x.shape
    grid = (nseq // sb, slen // tb)
    (g_attn, wq, gqa, wqb, gq, eq, wkv, gkva, wmisc, gkr, wf, gfq, gfk, ef, bfv) = consts
    tqc, tqs1, tqs2, tkc, tks1, tks2 = tables

    def tok(width):
        return pl.BlockSpec((sb, tb, width), lambda i, j: (i, j, 0))

    mod_spec = pl.BlockSpec((sb, 1, d), lambda i, j: (i, 0, 0))
    tab_spec = pl.BlockSpec((1, tb, LANES), lambda i, j: (0, j, 0))
    cs = _const_spec
    args = [x, shift, scale, g_attn, wq, gqa, wqb, gq, eq, tqc, tqs1, tqs2,
            wkv, gkva, wmisc, gkr, tkc, tks1, tks2, wf, gfq, gfk, ef, bfv]
    in_specs = [tok(d), mod_spec, mod_spec, cs(g_attn.shape),
                cs(wq.shape), cs(gqa.shape), cs(wqb.shape), cs(gq.shape), cs(eq.shape),
                tab_spec, tab_spec, tab_spec,
                cs(wkv.shape), cs(gkva.shape), cs(wmisc.shape), cs(gkr.shape), tab_spec, tab_spec, tab_spec,
                cs(wf.shape), cs(gfq.shape), cs(gfk.shape), cs(ef.shape), cs(bfv.shape)]
    kv_lora = wkv.shape[1]
    fw = wf.shape[1] // 3
    n_fh = fw // FOX_DIM
    qw = wqb.shape[1]

    def sds(width, dtype):
        return jax.ShapeDtypeStruct((nseq, slen, width), dtype)

    out_shape = [sds(kv_lora, F32), sds(MLA_ROPE, F32), sds(fw, F32), sds(fw, F32), sds(n_fh, F32),
                 sds(qw, BF16)]
    out_specs = [tok(kv_lora), tok(MLA_ROPE), tok(fw), tok(fw), tok(n_fh), tok(qw)]
    scratch = []
    if prompt:
        assert sb == 1
        tril = jnp.asarray(np.tril(np.ones((tb, tb), np.float32)), BF16)
        pc = list(prompt_consts[:8]) + [tril] + list(prompt_consts[8:])
        args += pc
        in_specs += [cs(a.shape) for a in pc]
        vrows = prompt_consts[4].shape[0]
        vt_spec = pl.BlockSpec((1, vrows, tb), lambda i, j: (i, 0, j))
        vt_sds = jax.ShapeDtypeStruct((nseq, vrows, slen), BF16)
        out_shape += [sds(qw, BF16), vt_sds, sds(n_fh * LANES, BF16), sds(n_fh * LANES, BF16), vt_sds]
        out_specs += [tok(qw), vt_spec, tok(n_fh * LANES), tok(n_fh * LANES), vt_spec]
        scratch = [pltpu.VMEM((1, LANES), F32)]
    else:
        out_shape += [sds(fw, BF16)]
        out_specs += [tok(fw)]
    return pl.pallas_call(
        functools.partial(_token_kernel, prompt=prompt),
        grid=grid, in_specs=in_specs, out_specs=out_specs, out_shape=out_shape,
        scratch_shapes=scratch,
        compiler_params=pltpu.CompilerParams(
            dimension_semantics=("parallel", "arbitrary"), vmem_limit_bytes=VMEM_LIMIT),
        name="token_prompt" if prompt else "token_sample",
    )(*args)


def _cumsum_pages_kernel(lt_ref, triu_ref, o_ref):
    r, nh, p = lt_ref.shape
    triu = triu_ref[...]
    hi, mid, lo = _split3(lt_ref[...].reshape(r * nh, p))
    c = (jnp.dot(hi, triu, preferred_element_type=F32)
         + jnp.dot(mid, triu, preferred_element_type=F32)
         + jnp.dot(lo, triu, preferred_element_type=F32))
    o_ref[...] = c.reshape(r, nh, p)


def _cumsum_pages_call(pool_t, rows):
    n_phys, nh, p = pool_t.shape
    triu = jnp.asarray(np.triu(np.ones((p, p), np.float32)), BF16)
    return pl.pallas_call(
        _cumsum_pages_kernel,
        grid=(n_phys // rows,),
        in_specs=[pl.BlockSpec((rows, nh, p), lambda i: (i, 0, 0)), _const_spec((p, p))],
        out_specs=pl.BlockSpec((rows, nh, p), lambda i: (i, 0, 0)),
        out_shape=jax.ShapeDtypeStruct((n_phys, nh, p), F32),
        compiler_params=pltpu.CompilerParams(dimension_semantics=("parallel",)),
        name="cumsum_pages",
    )(pool_t, triu)


def _flash_kernel(q_ref, k_ref, vt_ref, o_ref, m_sc, acc_sc):
    qi = pl.program_id(2)
    ki = pl.program_id(3)
    tq = q_ref.shape[1]
    tk = k_ref.shape[1]
    cq = min(FLASH_CQ, tq)
    ck = min(FLASH_CK, tk)

    @pl.when(ki == 0)
    def _():
        m_sc[...] = jnp.full_like(m_sc, NEG_INF)
        acc_sc[...] = jnp.zeros_like(acc_sc)

    def step(masked):
        for hh in range(2):
            for q0 in range(0, tq, cq):
                q = q_ref[0, q0:q0 + cq, hh * LANES:(hh + 1) * LANES]
                m = m_sc[hh, :, q0:q0 + cq]
                acc = acc_sc[hh, :, q0:q0 + cq]
                for k0 in range(0, tk, ck):
                    if masked and k0 > q0 + cq - 1:
                        continue
                    k = k_ref[0, k0:k0 + ck, hh * LANES:(hh + 1) * LANES]
                    st = _dot_nt(k, q)
                    if masked and k0 + ck - 1 > q0:
                        key = k0 + lax.broadcasted_iota(jnp.int32, (ck, cq), 0)
                        qry = q0 + lax.broadcasted_iota(jnp.int32, (ck, cq), 1)
                        st = jnp.where(key <= qry, st, NEG_INF)
                    st3 = st.reshape(ck // SUBLANES, SUBLANES, cq)
                    m_cur = jnp.max(jnp.max(st3, axis=0), axis=0, keepdims=True)
                    m_new = jnp.maximum(m, m_cur)
                    alpha = jnp.exp2(m - m_new)
                    p = jnp.exp2(st3 - m_new[None]).reshape(ck, cq)
                    vt = vt_ref[0, hh * V_ROWS:(hh + 1) * V_ROWS, k0:k0 + ck]
                    acc = (acc.reshape(V_ROWS // SUBLANES, SUBLANES, cq) * alpha[None]).reshape(V_ROWS, cq)
                    acc = acc + _dot(vt, p)
                    m = m_new
                m_sc[hh, :, q0:q0 + cq] = m
                acc_sc[hh, :, q0:q0 + cq] = acc

    @pl.when(ki < qi)
    def _():
        step(False)

    @pl.when(ki == qi)
    def _():
        step(True)
        outs = []
        for hh in range(2):
            acc = acc_sc[hh]
            outs.append(acc[:MLA_V, :] / acc[MLA_V:MLA_V + 1, :])
        o_ref[0] = jnp.concatenate(outs, axis=0).T.astype(o_ref.dtype)


def _flash_call(q, k, vt, *, tile, name):
    b, s, qw_all = q.shape
    hp = vt.shape[1] // (2 * V_ROWS)
    qw = qw_all // hp
    assert qw == 2 * LANES
    n_t = s // tile
    grid = (b, hp, n_t, n_t)
    in_specs = [pl.BlockSpec((1, tile, qw), lambda bi, h, qi, ki: (bi, qi, h)),
                pl.BlockSpec((1, tile, qw), lambda bi, h, qi, ki: (bi, jnp.minimum(ki, qi), h)),
                pl.BlockSpec((1, 2 * V_ROWS, tile), lambda bi, h, qi, ki: (bi, h, jnp.minimum(ki, qi)))]
    return pl.pallas_call(
        _flash_kernel,
        grid=grid, in_specs=in_specs,
        out_specs=pl.BlockSpec((1, tile, LANES), lambda bi, h, qi, ki: (bi, qi, h)),
        out_shape=jax.ShapeDtypeStruct((b, s, hp * LANES), BF16),
        scratch_shapes=[pltpu.VMEM((2, SUBLANES, tile), F32), pltpu.VMEM((2, V_ROWS, tile), F32)],
        compiler_params=pltpu.CompilerParams(
            dimension_semantics=("parallel", "parallel", "parallel", "arbitrary"),
            vmem_limit_bytes=VMEM_LIMIT),
        name=name,
    )(q, k, vt)


def _head_rows(x, n_heads, width):
    lane = lax.broadcasted_iota(jnp.int32, x.shape, 1)
    return jnp.concatenate(
        [jnp.where((lane >= h * width) & (lane < (h + 1) * width), x, jnp.zeros_like(x))
         for h in range(n_heads)], axis=0)


def _head_diag(full, n_heads, width, t):
    lane = lax.broadcasted_iota(jnp.int32, (t, full.shape[1]), 1)
    out = jnp.zeros((t, full.shape[1]), full.dtype)
    for h in range(n_heads):
        blk = full[h * t:(h + 1) * t, :]
        out = out + jnp.where((lane >= h * width) & (lane < (h + 1) * width), blk, 0.0)
    return out


def _online_update(s, pv, m_sc, l_sc, acc_sc, first):
    if first:
        m_new = jnp.max(s, axis=1, keepdims=True)
        p = jnp.exp2(s - m_new)
        l_sc[...] = jnp.sum(p, axis=1, keepdims=True)
        acc_sc[...] = pv(p)
    else:
        m_prev = m_sc[...]
        m_new = jnp.maximum(m_prev, jnp.max(s, axis=1, keepdims=True))
        alpha = jnp.exp2(m_prev - m_new)
        p = jnp.exp2(s - m_new)
        l_sc[...] = alpha * l_sc[...] + jnp.sum(p, axis=1, keepdims=True)
        acc_sc[...] = alpha * acc_sc[...] + pv(p)
    m_sc[...] = m_new


def _new_key_mask(rows, t):
    row = lax.broadcasted_iota(jnp.int32, (rows, PAGE), 0)
    col = lax.broadcasted_iota(jnp.int32, (rows, PAGE), 1)
    return col <= row % t


def _pad_rows(x, rows):
    return jnp.concatenate([x, jnp.zeros((rows - x.shape[0], x.shape[1]), x.dtype)], axis=0)


def _mla_decode_kernel(pt_ref, q_ref, cnew_ref, krnew_ref, sel_ref, gkn_ref, wkn_ref, wv_ref, es_ref,
                       *rest, n_pp, n_heads):
    ckv_refs = rest[:n_pp]
    krt_refs = rest[n_pp:2 * n_pp]
    o_ref, qbd_sc, qr_sc, m_sc, l_sc, acc_sc = rest[2 * n_pp:]
    j = pl.program_id(1)
    t = q_ref.shape[1]

    def nope_scores(ckv_b):
        kraw = _dot(ckv_b, wkn_ref[...])
        s = _dot_nt(qbd_sc[...], kraw)
        ssq = _dot_nt(es_ref[...], kraw * kraw)
        return s * lax.rsqrt(ssq + EPS)

    @pl.when(j == 0)
    def _():
        qm = q_ref[0].astype(F32)
        qn = _dot(qm, sel_ref[...]) * gkn_ref[...]
        qbd_sc[...] = _head_rows(qn, n_heads, MLA_NOPE).astype(BF16)
        qr_sc[...] = jnp.concatenate(
            [qm[:, h * LANES + MLA_NOPE:h * LANES + MLA_NOPE + MLA_ROPE] for h in range(n_heads)],
            axis=0).astype(BF16)
        ckv_b = _pad_rows(cnew_ref[0], PAGE).astype(BF16)
        s = nope_scores(ckv_b) + _dot_nt(qr_sc[...], _pad_rows(krnew_ref[0], PAGE))
        s = jnp.where(_new_key_mask(n_heads * t, t), s, NEG_INF)
        _online_update(s, lambda p: _dot(p, ckv_b), m_sc, l_sc, acc_sc, first=True)

    ckv_b = jnp.concatenate([r[0].astype(BF16) for r in ckv_refs], axis=0)
    krt = jnp.concatenate([r[0].astype(BF16) for r in krt_refs], axis=1)
    s = nope_scores(ckv_b) + _dot(qr_sc[...], krt)
    _online_update(s, lambda p: _dot(p, ckv_b), m_sc, l_sc, acc_sc, first=False)

    @pl.when(j == pl.num_programs(1) - 1)
    def _():
        o_lat = acc_sc[...] / l_sc[...]
        full = _dot(o_lat, wv_ref[...])
        o_ref[0] = _head_diag(full, n_heads, MLA_V, t).astype(o_ref.dtype)


def _page_map(slot, n_pp, n_pages, nd):
    def index_map(b, j, pt):
        logical = n_pages - 1 - (j * n_pp + slot)
        return (pt[b * n_pages + logical],) + (0,) * (nd - 1)
    return index_map


def _mla_decode_call(pt_flat, q, ckv_new, kr_new, sel, gkn, wkn, wv, es, pool_ckv, pool_krt, *, n_pp):
    bd, t, qw = q.shape
    n_pages = pt_flat.shape[0] // bd
    n_heads = qw // LANES
    kv_lora = pool_ckv.shape[-1]

    def batch(shape):
        return pl.BlockSpec((1,) + shape, lambda b, j, pt: (b, 0, 0))

    cs = _const_spec
    in_specs = [batch((t, qw)), batch((t, kv_lora)), batch((t, MLA_ROPE)),
                cs(sel.shape), cs(gkn.shape), cs(wkn.shape), cs(wv.shape), cs(es.shape)]
    in_specs += [pl.BlockSpec((1, PAGE, kv_lora), _page_map(p, n_pp, n_pages, 3)) for p in range(n_pp)]
    in_specs += [pl.BlockSpec((1, MLA_ROPE, PAGE), _page_map(p, n_pp, n_pages, 3)) for p in range(n_pp)]
    ow = wv.shape[1]
    rows = n_heads * t
    grid_spec = pltpu.PrefetchScalarGridSpec(
        num_scalar_prefetch=1, grid=(bd, n_pages // n_pp), in_specs=in_specs,
        out_specs=pl.BlockSpec((1, t, ow), lambda b, j, pt: (b, 0, 0)),
        scratch_shapes=[pltpu.VMEM((rows, wkn.shape[1]), BF16), pltpu.VMEM((rows, MLA_ROPE), BF16),
                        pltpu.VMEM((rows, 1), F32), pltpu.VMEM((rows, 1), F32),
                        pltpu.VMEM((rows, kv_lora), F32)])
    return pl.pallas_call(
        functools.partial(_mla_decode_kernel, n_pp=n_pp, n_heads=n_heads),
        grid_spec=grid_spec,
        out_shape=jax.ShapeDtypeStruct((bd, t, ow), BF16),
        compiler_params=pltpu.CompilerParams(
            dimension_semantics=("parallel", "arbitrary"), vmem_limit_bytes=VMEM_LIMIT),
        name="mla_decode",
    )(pt_flat, q, ckv_new, kr_new, sel, gkn, wkn, wv, es, *([pool_ckv] * n_pp), *([pool_krt] * n_pp))


def _fox_decode_kernel(pt_ref, q_ref, knew_ref, vnew_ref, lnew_ref, lnewt_ref, *rest, n_pp, n_heads):
    kt_refs = rest[:n_pp]
    vt_refs = rest[n_pp:2 * n_pp]
    c_refs = rest[2 * n_pp:3 * n_pp]
    o_ref, qbd_sc, acol_sc, suf_sc, m_sc, l_sc, acc_sc = rest[3 * n_pp:]
    j = pl.program_id(1)
    t = q_ref.shape[1]

    def head_rows_bias(x):
        return jnp.concatenate(
            [jnp.broadcast_to(x[h:h + 1, :], (t, x.shape[1])) for h in range(n_heads)], axis=0)

    @pl.when(j == 0)
    def _():
        qbd_sc[...] = _head_rows(q_ref[0].astype(F32), n_heads, FOX_DIM).astype(BF16)
        lnew = lnew_ref[0] * LOG2E
        row = lax.broadcasted_iota(jnp.int32, lnew.shape, 0)
        cnew = jnp.zeros_like(lnew)
        for tt in range(t):
            cnew = cnew + jnp.where(row >= tt, lnew[tt:tt + 1, :], 0.0)
        for h in range(n_heads):
            acol_sc[h * t:(h + 1) * t, :] = cnew[:, h:h + 1]
        lnt = lnewt_ref[0] * LOG2E
        lane = lax.broadcasted_iota(jnp.int32, lnt.shape, 1)
        cnew_t = jnp.zeros_like(lnt)
        for tt in range(t):
            cnew_t = cnew_t + jnp.where(lane >= tt, lnt[:, tt:tt + 1], 0.0)
        suf_sc[...] = jnp.zeros_like(suf_sc)
        k_b = _pad_rows(knew_ref[0], PAGE).astype(BF16)
        v_b = _pad_rows(vnew_ref[0], PAGE).astype(BF16)
        s = _dot_nt(qbd_sc[...], k_b) + (acol_sc[...] - head_rows_bias(cnew_t))
        s = jnp.where(_new_key_mask(n_heads * t, t), s, NEG_INF)
        _online_update(s, lambda p: _dot(p, v_b), m_sc, l_sc, acc_sc, first=True)

    suffix = suf_sc[...]
    biases = []
    for p in range(n_pp):
        c = c_refs[p][0] * LOG2E
        total = c[:, PAGE - 1:PAGE]
        biases.append(suffix + (total - c))
        suffix = suffix + total
    suf_sc[...] = suffix
    bias = biases[0] if n_pp == 1 else jnp.concatenate(biases, axis=1)
    kt_b = jnp.concatenate([r[0].astype(BF16) for r in kt_refs], axis=1)
    vt_b = jnp.concatenate([r[0].astype(BF16) for r in vt_refs], axis=1)
    s = _dot(qbd_sc[...], kt_b) + (acol_sc[...] + head_rows_bias(bias))
    _online_update(s, lambda p: _dot_nt(p, vt_b), m_sc, l_sc, acc_sc, first=False)

    @pl.when(j == pl.num_programs(1) - 1)
    def _():
        o_ref[0] = _head_diag(acc_sc[...] / l_sc[...], n_heads, FOX_DIM, t).astype(o_ref.dtype)


def _fox_decode_call(pt_flat, q, k_new, v_new, l_new, l_new_t, pool_kt, pool_vt, pool_c, *, n_pp):
    bd, t, fw = q.shape
    n_pages = pt_flat.shape[0] // bd
    n_heads = fw // FOX_DIM

    def batch(shape):
        return pl.BlockSpec((1,) + shape, lambda b, j, pt: (b, 0, 0))

    in_specs = [batch((t, fw)), batch((t, fw)), batch((t, fw)), batch((t, n_heads)), batch((n_heads, PAGE))]
    in_specs += [pl.BlockSpec((1, fw, PAGE), _page_map(p, n_pp, n_pages, 3)) for p in range(n_pp)]
    in_specs += [pl.BlockSpec((1, fw, PAGE), _page_map(p, n_pp, n_pages, 3)) for p in range(n_pp)]
    in_specs += [pl.BlockSpec((1, n_heads, PAGE), _page_map(p, n_pp, n_pages, 3)) for p in range(n_pp)]
    rows = n_heads * t
    grid_spec = pltpu.PrefetchScalarGridSpec(
        num_scalar_prefetch=1, grid=(bd, n_pages // n_pp), in_specs=in_specs,
        out_specs=pl.BlockSpec((1, t, fw), lambda b, j, pt: (b, 0, 0)),
        scratch_shapes=[pltpu.VMEM((rows, fw), BF16), pltpu.VMEM((rows, 1), F32),
                        pltpu.VMEM((n_heads, 1), F32),
                        pltpu.VMEM((rows, 1), F32), pltpu.VMEM((rows, 1), F32),
                        pltpu.VMEM((rows, fw), F32)])
    return pl.pallas_call(
        functools.partial(_fox_decode_kernel, n_pp=n_pp, n_heads=n_heads),
        grid_spec=grid_spec,
        out_shape=jax.ShapeDtypeStruct((bd, t, fw), BF16),
        compiler_params=pltpu.CompilerParams(
            dimension_semantics=("parallel", "arbitrary"), vmem_limit_bytes=VMEM_LIMIT),
        name="fox_decode",
    )(pt_flat, q, k_new, v_new, l_new, l_new_t,
      *([pool_kt] * n_pp), *([pool_vt] * n_pp), *([pool_c] * n_pp))


def _out_mlp_kernel(x_ref, om_ref, of_ref, ga_ref, shm_ref, scm_ref, gm_ref, gmlp_ref,
                    wom_ref, wof_ref, w1_ref, w2_ref, y_ref, *, ff_chunk):
    sb, tb, d = x_ref.shape
    tm = sb * tb
    o = (_dot(om_ref[...].reshape(tm, om_ref.shape[-1]), wom_ref[...])
         + _dot(of_ref[...].reshape(tm, of_ref.shape[-1]), wof_ref[...]))
    x1 = x_ref[...] + ga_ref[...] * o.reshape(sb, tb, d)
    ms = jnp.mean(x1 * x1, axis=-1, keepdims=True)
    h = x1 * lax.rsqrt(ms + EPS) * gmlp_ref[...]
    h = h * (1.0 + scm_ref[...]) + shm_ref[...]
    hb = h.reshape(tm, d).astype(BF16)
    d_ff = w1_ref.shape[1]
    acc = jnp.zeros((tm, d), F32)
    for c in range(d_ff // ff_chunk):
        u = jnp.maximum(_dot(hb, w1_ref[:, c * ff_chunk:(c + 1) * ff_chunk]), 0.0)
        acc = acc + _dot(u * u, w2_ref[c * ff_chunk:(c + 1) * ff_chunk, :])
    y_ref[...] = x1 + gm_ref[...] * acc.reshape(sb, tb, d)


def _out_mlp_call(x, o_mla, o_fox, gate_a, shift_m, scale_m, gate_m, g_mlp, wom, wof, w1, w2, *, sb, tb):
    nseq, slen, d = x.shape

    def tok(width):
        return pl.BlockSpec((sb, tb, width), lambda i, j: (i, j, 0))

    mod_spec = pl.BlockSpec((sb, 1, d), lambda i, j: (i, 0, 0))

    def weight(shape):
        return pl.BlockSpec(shape, lambda i, j: (0, 0), pipeline_mode=pl.Buffered(1))

    return pl.pallas_call(
        functools.partial(_out_mlp_kernel, ff_chunk=min(1024, w1.shape[1])),
        grid=(nseq // sb, slen // tb),
        in_specs=[tok(d), tok(o_mla.shape[-1]), tok(o_fox.shape[-1]),
                  mod_spec, mod_spec, mod_spec, mod_spec, _const_spec(g_mlp.shape),
                  weight(wom.shape), weight(wof.shape), weight(w1.shape), weight(w2.shape)],
        out_specs=tok(d),
        out_shape=jax.ShapeDtypeStruct((nseq, slen, d), F32),
        compiler_params=pltpu.CompilerParams(
            dimension_semantics=("parallel", "parallel"), vmem_limit_bytes=VMEM_LIMIT),
        name="out_mlp",
    )(x, o_mla, o_fox, gate_a, shift_m, scale_m, gate_m, g_mlp, wom, wof, w1, w2)


def _rope_tables(pos, scale_q):
    half = MLA_ROPE // 2
    inv_freq = ROPE_THETA ** (-jnp.arange(half, dtype=F32) / half)
    ang = pos.astype(F32)[:, None] * inv_freq[None, :]
    cos, sin = jnp.cos(ang), jnp.sin(ang)
    n = pos.shape[0]
    z = lambda w: jnp.zeros((n, w), F32)
    one = jnp.ones((n, MLA_NOPE), F32)
    tqc = jnp.concatenate([one, cos, cos, z(LANES - MLA_NOPE - MLA_ROPE)], axis=1) * scale_q
    tqs1 = jnp.concatenate([z(MLA_NOPE), -sin, z(LANES - MLA_NOPE - half)], axis=1) * scale_q
    tqs2 = jnp.concatenate([z(MLA_NOPE + half), sin, z(LANES - MLA_NOPE - MLA_ROPE)], axis=1) * scale_q
    tkc = jnp.concatenate([cos, cos, z(LANES - MLA_ROPE)], axis=1)
    tks1 = jnp.concatenate([-sin, z(LANES - half)], axis=1)
    tks2 = jnp.concatenate([z(half), sin, z(LANES - MLA_ROPE)], axis=1)
    return tuple(t[None] for t in (tqc, tqs1, tqs2, tkc, tks1, tks2))


def _layer_consts(w_in, b_f, g_attn, g_qa, w_qb, g_qn, g_qr, g_kva, g_kr, w_kvb, g_kn, g_fq, g_fk):
    d = w_in.shape[0]
    q_lora, n_mh, _ = w_qb.shape
    kv_lora = w_kvb.shape[0]
    n_fh = b_f.shape[0]
    fw = n_fh * FOX_DIM
    c0, c1, c2 = q_lora, q_lora + kv_lora, q_lora + kv_lora + MLA_ROPE
    c3 = c2 + 3 * fw
    pad_h = LANES - MLA_NOPE - MLA_ROPE
    row = lambda v: v.reshape(1, -1).astype(F32)

    wq = w_in[:, :c0].astype(BF16)
    wkv = w_in[:, c0:c1].astype(BF16)
    wmisc = jnp.concatenate([w_in[:, c1:c2], w_in[:, c3:c3 + n_fh],
                             jnp.zeros((d, LANES - MLA_ROPE - n_fh), F32)], axis=1).astype(BF16)
    wf = w_in[:, c2:c3].astype(BF16)
    wqb = jnp.pad(w_qb, ((0, 0), (0, 0), (0, pad_h))).reshape(q_lora, n_mh * LANES).astype(BF16)
    gq = row(jnp.concatenate([g_qn, g_qr, jnp.zeros((n_mh, pad_h), F32)], axis=1))
    wkn = jnp.pad(w_kvb[:, :, :MLA_NOPE], ((0, 0), (0, 0), (0, LANES - MLA_NOPE)))
    wkn = wkn.reshape(kv_lora, n_mh * LANES).astype(BF16)
    gk = row(jnp.pad(g_kn, ((0, 0), (0, LANES - MLA_NOPE))))
    wkn_c = w_kvb[:, :, :MLA_NOPE].reshape(kv_lora, n_mh * MLA_NOPE).astype(BF16)
    wv_c = w_kvb[:, :, MLA_NOPE:].reshape(kv_lora, n_mh * MLA_V).astype(BF16)
    gkr = row(jnp.pad(g_kr, (0, LANES - MLA_ROPE)))
    bfv = row(jnp.concatenate([jnp.zeros((LOGF_LANE,), F32), b_f, jnp.zeros((LANES - LOGF_LANE - n_fh,), F32)]))
    gfq = row(g_fq) * (FOX_DIM ** -0.5 * LOG2E)
    gfk = row(g_fk)

    pad_v = V_ROWS - MLA_V
    wvt = jnp.pad(jnp.transpose(w_kvb[:, :, MLA_NOPE:], (1, 2, 0)), ((0, 0), (0, pad_v), (0, 0)))
    wvt = wvt.reshape(n_mh * V_ROWS, kv_lora).astype(BF16)
    wfv = w_in[:, c2 + 2 * fw:c3].reshape(d, n_fh, FOX_DIM)
    wfvt = jnp.pad(jnp.transpose(wfv, (1, 2, 0)), ((0, 0), (0, pad_v), (0, 0)))
    wfvt = wfvt.reshape(n_fh * V_ROWS, d).astype(BF16)
    assert n_mh == n_fh
    ones_col = np.zeros((n_mh, V_ROWS, 1), np.float32)
    ones_col[:, MLA_V:, :] = 1.0
    ones_col = jnp.asarray(ones_col.reshape(n_mh * V_ROWS, 1))

    idx = np.arange(LANES)
    nope = idx < MLA_NOPE
    rope = (idx >= MLA_NOPE) & (idx < MLA_NOPE + MLA_ROPE)
    eq = (np.outer(nope, nope) / MLA_NOPE + np.outer(rope, rope) / MLA_ROPE).astype(np.float32)
    ek = (np.outer(nope, nope) / MLA_NOPE).astype(np.float32)
    ef = ((idx[:, None] // FOX_DIM) == (idx[None, :] // FOX_DIM)).astype(np.float32) / FOX_DIM
    rep = np.zeros((LANES, n_mh * LANES), np.float32)
    sel = np.zeros((n_mh * LANES, n_mh * MLA_NOPE), np.float32)
    placef = np.zeros((fw, n_fh * LANES), np.float32)
    placeq = np.zeros((LANES, n_fh * LANES), np.float32)
    placek = np.zeros((LANES, n_fh * LANES), np.float32)
    constq = np.zeros((1, n_fh * LANES), np.float32)
    constk = np.zeros((1, n_fh * LANES), np.float32)
    for h in range(n_mh):
        rep[np.arange(MLA_ROPE), h * LANES + MLA_NOPE + np.arange(MLA_ROPE)] = 1.0
        sel[h * LANES + np.arange(MLA_NOPE), h * MLA_NOPE + np.arange(MLA_NOPE)] = 1.0
    for h in range(n_fh):
        placef[h * FOX_DIM + np.arange(FOX_DIM), h * LANES + np.arange(FOX_DIM)] = 1.0
        for part in range(3):
            placeq[LOGF_LANE + part * n_fh + h, h * LANES + BIAS_LANE + part] = 1.0
            placek[LOGF_LANE + part * n_fh + h, h * LANES + BIAS_LANE + 3 + part] = -1.0
            constq[0, h * LANES + BIAS_LANE + 3 + part] = 1.0
            constk[0, h * LANES + BIAS_LANE + part] = 1.0
    t_dec = SUBLANES
    es = np.zeros((n_mh * t_dec, n_mh * MLA_NOPE), np.float32)
    for h in range(n_mh):
        es[h * t_dec:(h + 1) * t_dec, h * MLA_NOPE:(h + 1) * MLA_NOPE] = 1.0 / MLA_NOPE

    bf = lambda a: jnp.asarray(a, BF16)
    token_consts = (row(g_attn), wq, row(g_qa), wqb, gq, bf(eq), wkv, row(g_kva),
                    wmisc, gkr, wf, gfq, gfk, bf(ef), bfv)
    prompt_consts = (wkn, gk, bf(ek), bf(rep), wvt, wfvt, ones_col, bf(placef),
                     bf(placeq), bf(placek), jnp.asarray(constq), jnp.asarray(constk))
    decode_consts = (bf(sel), row(g_kn), wkn_c, wv_c, bf(es))
    return token_consts, prompt_consts, decode_consts


def kernel(x_prompt, x_sample, cache_mla_ckv, cache_mla_krope, cache_fox_k, cache_fox_v, cache_fox_logf,
           page_table, c_prompt, c_sample, w_ada, b_ada, g_attn, g_mlp, w_in, b_f, g_qa, w_qb, g_qn,
           g_qr, g_kva, g_kr, w_kvb, g_kn, g_fq, g_fk, w_o, w_mlp1, w_mlp2):
    depth = w_ada.shape[0]
    b, s, d = x_prompt.shape
    bd, t_dec, _ = x_sample.shape
    assert t_dec == SUBLANES, "decode kernels lay the new tokens on one sublane tile"
    n_pages = page_table.shape[1]
    past_len = n_pages * PAGE
    n_phys = cache_mla_ckv.shape[1]
    n_fh = b_f.shape[1]
    fw = n_fh * FOX_DIM
    n_mh = w_qb.shape[2]
    assert n_fh % 2 == 0 and n_mh % 2 == 0
    assert 3 * n_fh <= LANES - LOGF_LANE

    tile_p = min(512, s)
    sb_s = min(64, bd)
    n_pp = min(16, n_pages)
    cum_rows = math.gcd(n_phys, 256)
    assert s % tile_p == 0 and bd % sb_s == 0 and n_pages % n_pp == 0

    scale_mla = (MLA_NOPE + MLA_ROPE) ** -0.5 * LOG2E
    tabs_p = _rope_tables(jnp.arange(s, dtype=jnp.int32), scale_mla)
    tabs_s = _rope_tables(past_len + jnp.arange(t_dec, dtype=jnp.int32), scale_mla)
    pt_flat = page_table.reshape(-1).astype(jnp.int32)

    y_p, y_s = x_prompt, x_sample
    rows_p, rows_s = [], []
    for l in range(depth):
        token_consts, prompt_consts, decode_consts = _layer_consts(
            w_in[l], b_f[l], g_attn[l], g_qa[l], w_qb[l], g_qn[l], g_qr[l], g_kva[l], g_kr[l],
            w_kvb[l], g_kn[l], g_fq[l], g_fk[l])
        mod = _ada_call(jnp.concatenate([c_prompt, c_sample], axis=0), w_ada[l], b_ada[l])
        mods_p = [m[:, None, :] for m in jnp.split(mod[:b], 6, axis=-1)]
        mods_s = [m[:, None, :] for m in jnp.split(mod[b:], 6, axis=-1)]

        (p_ckv, p_krope, p_k, p_v, p_logf, q_mla, k_mla, vt_mla, fq_aug, fk_aug, vt_fox) = _token_call(
            y_p, mods_p[0], mods_p[1], token_consts, prompt_consts, tabs_p, prompt=True, sb=1, tb=tile_p)
        (s_ckv, s_krope, s_k, s_v, s_logf, q_mla_s, fq_s) = _token_call(
            y_s, mods_s[0], mods_s[1], token_consts, None, tabs_s, prompt=False, sb=sb_s, tb=t_dec)

        o_mla_p = _flash_call(q_mla, k_mla, vt_mla, tile=tile_p, name="mla_prompt")
        o_fox_p = _flash_call(fq_aug, fk_aug, vt_fox, tile=tile_p, name="fox_prompt")

        sel, gkn, wkn_c, wv_c, es = decode_consts
        pool_krt = jnp.swapaxes(cache_mla_krope[l], 1, 2)
        o_mla_s = _mla_decode_call(pt_flat, q_mla_s, s_ckv, s_krope, sel, gkn, wkn_c, wv_c, es,
                                   cache_mla_ckv[l], pool_krt, n_pp=n_pp)
        pool_kt = jnp.transpose(cache_fox_k[l], (0, 2, 3, 1)).reshape(n_phys, fw, PAGE)
        pool_vt = jnp.transpose(cache_fox_v[l], (0, 2, 3, 1)).reshape(n_phys, fw, PAGE)
        pool_c = _cumsum_pages_call(jnp.swapaxes(cache_fox_logf[l], 1, 2), cum_rows)
        l_new_t = jnp.pad(jnp.swapaxes(s_logf, 1, 2), ((0, 0), (0, 0), (0, PAGE - t_dec)))
        o_fox_s = _fox_decode_call(pt_flat, fq_s, s_k, s_v, s_logf, l_new_t,
                                   pool_kt, pool_vt, pool_c, n_pp=n_pp)

        wo = w_o[l].astype(BF16)
        wom, wof = wo[:n_mh * MLA_V], wo[n_mh * MLA_V:]
        w1 = w_mlp1[l].astype(BF16)
        w2 = w_mlp2[l].astype(BF16)
        g_mlp_row = g_mlp[l].reshape(1, -1)
        y_p = _out_mlp_call(y_p, o_mla_p, o_fox_p, mods_p[2], mods_p[3], mods_p[4], mods_p[5],
                            g_mlp_row, wom, wof, w1, w2, sb=1, tb=tile_p)
        y_s = _out_mlp_call(y_s, o_mla_s, o_fox_s, mods_s[2], mods_s[3], mods_s[4], mods_s[5],
                            g_mlp_row, wom, wof, w1, w2, sb=sb_s, tb=t_dec)

        rows_p.append((p_ckv, p_krope, p_k.reshape(b, s, n_fh, FOX_DIM), p_v.reshape(b, s, n_fh, FOX_DIM), p_logf))
        rows_s.append((s_ckv, s_krope, s_k.reshape(bd, t_dec, n_fh, FOX_DIM),
                       s_v.reshape(bd, t_dec, n_fh, FOX_DIM), s_logf))

    outs_p = [jnp.stack(t, axis=0) for t in zip(*rows_p)]
    outs_s = [jnp.stack(t, axis=0) for t in zip(*rows_s)]
    return (y_p, y_s, *outs_p, *outs_s)
```

```python
import functools
import math

import jax
import jax.numpy as jnp
import numpy as np
from jax import lax
from jax.experimental import pallas as pl
from jax.experimental.pallas import tpu as pltpu

F32 = jnp.float32
BF16 = jnp.bfloat16

LANES = 128
SUBLANES = 8
BF16_ROWS = 16
VMEM_LIMIT = 56 * 1024 * 1024

MLA_NOPE = 64
MLA_ROPE = 32
MLA_V = 64
FOX_DIM = 64
PAGE = 128
ROPE_THETA = 10000.0
EPS = 1e-6
NEG_INF = -1e30
LOG2E = math.log2(math.e)

V_ROWS = MLA_V + BF16_ROWS
FLASH_CQ = 512
FLASH_CK = 512
BIAS_LANE = FOX_DIM
LOGF_LANE = MLA_ROPE


def _dot(a, b):
    return jnp.dot(a.astype(BF16), b.astype(BF16), preferred_element_type=F32)


def _dot_nt(a, b):
    return lax.dot_general(a.astype(BF16), b.astype(BF16), (((1,), (1,)), ((), ())),
                           preferred_element_type=F32)


def _split3(x):
    hi = x.astype(BF16)
    r1 = x - hi.astype(F32)
    mid = r1.astype(BF16)
    lo = (r1 - mid.astype(F32)).astype(BF16)
    return hi, mid, lo


def _const_spec(shape):
    nd = len(shape)
    return pl.BlockSpec(shape, lambda *_, _nd=nd: (0,) * _nd)


def _group_mean(sq, e_ref):
    n = sq.shape[1] // LANES
    e = e_ref[...]
    parts = [_dot(sq[:, g * LANES:(g + 1) * LANES], e) for g in range(n)]
    return parts[0] if n == 1 else jnp.concatenate(parts, axis=1)


def _log_sigmoid(x):
    return jnp.minimum(x, 0.0) - jnp.log1p(jnp.exp(-jnp.abs(x)))


def _ada_kernel(c_ref, w_ref, b_ref, o_ref):
    c = c_ref[...]
    o_ref[...] = _dot(c * jax.nn.sigmoid(c), w_ref[...]) + b_ref[...]


def _ada_call(c, w_ada, b_ada):
    rows, d = c.shape
    n_chunks = w_ada.shape[1] // d
    return pl.pallas_call(
        _ada_kernel,
        grid=(n_chunks,),
        in_specs=[pl.BlockSpec((rows, d), lambda i: (0, 0)),
                  pl.BlockSpec((d, d), lambda i: (0, i)),
                  pl.BlockSpec((1, d), lambda i: (0, i))],
        out_specs=pl.BlockSpec((rows, d), lambda i: (0, i)),
        out_shape=jax.ShapeDtypeStruct((rows, n_chunks * d), F32),
        compiler_params=pltpu.CompilerParams(dimension_semantics=("arbitrary",)),
        name="adaln_mod",
    )(c, w_ada, b_ada.reshape(1, -1))


def _token_kernel(x_ref, sh_ref, sc_ref, gattn_ref,
                  wq_ref, gqa_ref, wqb_ref, gq_ref, eq_ref, tqc_ref, tqs1_ref, tqs2_ref,
                  wkv_ref, gkva_ref, wmisc_ref, gkr_ref, tkc_ref, tks1_ref, tks2_ref,
                  wf_ref, gfq_ref, gfk_ref, ef_ref, bf_ref,
                  *rest, prompt):
    if prompt:
        (wkn_ref, gk_ref, ek_ref, rep_ref, wvt_ref, wfvt_ref, ones_ref, placef_ref,
         tril_ref, placeq_ref, placek_ref, constq_ref, constk_ref,
         ckv_o, krope_o, fk_o, fv_o, logf_o, q_o,
         kmla_o, vtm_o, fqa_o, fka_o, vtf_o, carry_sc) = rest
    else:
        (ckv_o, krope_o, fk_o, fv_o, logf_o, q_o, fq_o) = rest
    sb, tb, d = x_ref.shape
    tm = sb * tb

    def to3(a):
        return a.reshape(sb, tb, a.shape[-1])

    def tiled(t_ref, width):
        t = t_ref[...]
        reps = width // t.shape[-1]
        return t if reps == 1 else jnp.concatenate([t] * reps, axis=-1)

    x = x_ref[...]
    ms = jnp.mean(x * x, axis=-1, keepdims=True)
    h = x * lax.rsqrt(ms + EPS) * gattn_ref[...]
    h = h * (1.0 + sc_ref[...]) + sh_ref[...]
    hb = h.reshape(tm, d).astype(BF16)

    zq = _dot(hb, wq_ref[...])
    ql = zq * lax.rsqrt(jnp.mean(zq * zq, axis=-1, keepdims=True) + EPS) * gqa_ref[...]
    qm = _dot(ql, wqb_ref[...])
    qn = qm * lax.rsqrt(_group_mean(qm * qm, eq_ref) + EPS) * gq_ref[...]
    wq_lanes = qn.shape[1]
    half = MLA_ROPE // 2
    q_rot = (to3(qn) * tiled(tqc_ref, wq_lanes)
             + to3(pltpu.roll(qn, wq_lanes - half, 1)) * tiled(tqs1_ref, wq_lanes)
             + to3(pltpu.roll(qn, half, 1)) * tiled(tqs2_ref, wq_lanes))
    q_o[...] = q_rot.astype(q_o.dtype)

    zkv = _dot(hb, wkv_ref[...])
    ckv = zkv * lax.rsqrt(jnp.mean(zkv * zkv, axis=-1, keepdims=True) + EPS) * gkva_ref[...]
    ckv_o[...] = to3(ckv)

    zm = _dot(hb, wmisc_ref[...])
    lane = lax.broadcasted_iota(jnp.int32, zm.shape, 1)
    kr = jnp.where(lane < MLA_ROPE, zm, 0.0)
    krn = kr * lax.rsqrt(jnp.sum(kr * kr, axis=-1, keepdims=True) * (1.0 / MLA_ROPE) + EPS) * gkr_ref[...]
    kro = (to3(krn) * tkc_ref[...]
           + to3(pltpu.roll(krn, LANES - half, 1)) * tks1_ref[...]
           + to3(pltpu.roll(krn, half, 1)) * tks2_ref[...])
    krope_o[...] = kro[:, :, :MLA_ROPE]
    logf = _log_sigmoid(zm + bf_ref[...])
    n_fh = logf_o.shape[-1]
    logf_o[...] = to3(pltpu.roll(logf, LANES - LOGF_LANE, 1))[:, :, :n_fh]

    zf = _dot(hb, wf_ref[...])
    fw = zf.shape[1] // 3
    fq_raw, fk_raw, fv = zf[:, :fw], zf[:, fw:2 * fw], zf[:, 2 * fw:]
    fq = fq_raw * lax.rsqrt(_group_mean(fq_raw * fq_raw, ef_ref) + EPS) * gfq_ref[...]
    fk = fk_raw * lax.rsqrt(_group_mean(fk_raw * fk_raw, ef_ref) + EPS) * gfk_ref[...]
    fk_o[...] = to3(fk)
    fv_o[...] = to3(fv)

    if not prompt:
        fq_o[...] = to3(fq).astype(fq_o.dtype)
        return

    ckv_b = ckv.astype(BF16)
    kraw = _dot(ckv_b, wkn_ref[...])
    kn = kraw * lax.rsqrt(_group_mean(kraw * kraw, ek_ref) + EPS) * gk_ref[...]
    kmla = kn + _dot(kro.reshape(tm, LANES), rep_ref[...])
    kmla_o[0] = kmla.astype(kmla_o.dtype)
    ones_col = ones_ref[...]
    vtm_o[0] = (_dot_nt(wvt_ref[...], ckv_b) + ones_col).astype(vtm_o.dtype)
    vtf_o[0] = (_dot_nt(wfvt_ref[...], hb) + ones_col).astype(vtf_o.dtype)

    @pl.when(pl.program_id(1) == 0)
    def _():
        carry_sc[...] = jnp.zeros_like(carry_sc)

    lw = jnp.where((lane >= LOGF_LANE) & (lane < LOGF_LANE + n_fh), logf, 0.0)
    tril = tril_ref[...]
    hi, mid, lo = _split3(lw)
    cum = (jnp.dot(tril, hi, preferred_element_type=F32) + jnp.dot(tril, mid, preferred_element_type=F32)
           + jnp.dot(tril, lo, preferred_element_type=F32)) + carry_sc[...]
    carry_sc[...] = cum[tm - 1:, :]
    hi, mid, lo = _split3(cum * LOG2E)
    parts = (hi.astype(F32) + pltpu.roll(mid.astype(F32), n_fh, 1)
             + pltpu.roll(lo.astype(F32), 2 * n_fh, 1)).astype(BF16)
    fqa = _dot(fq, placef_ref[...]) + _dot(parts, placeq_ref[...]) + constq_ref[...]
    fka = _dot(fk, placef_ref[...]) + _dot(parts, placek_ref[...]) + constk_ref[...]
    fqa_o[0] = fqa.astype(fqa_o.dtype)
    fka_o[0] = fka.astype(fka_o.dtype)


def _token_call(x, shift, scale, consts, prompt_consts, tables, *, prompt, sb, tb):
    nseq, slen, d = x.shape
    grid = (nseq // sb, slen // tb)
    (g_attn, wq, gqa, wqb, gq, eq, wkv, gkva, wmisc, gkr, wf, gfq, gfk, ef, bfv) = consts
    tqc, tqs1, tqs2, tkc, tks1, tks2 = tables

    def tok(width):
        return pl.BlockSpec((sb, tb, width), lambda i, j: (i, j, 0))

    mod_spec = pl.BlockSpec((sb, 1, d), lambda i, j: (i, 0, 0))
    tab_spec = pl.BlockSpec((1, tb, LANES), lambda i, j: (0, j, 0))
    cs = _const_spec
    args = [x, shift, scale, g_attn, wq, gqa, wqb, gq, eq, tqc, tqs1, tqs2,
            wkv, gkva, wmisc, gkr, tkc, tks1, tks2, wf, gfq, gfk, ef, bfv]
    in_specs = [tok(d), mod_spec, mod_spec, cs(g_attn.shape),
                cs(wq.shape), cs(gqa.shape), cs(wqb.shape), cs(gq.shape), cs(eq.shape),
                tab_spec, tab_spec, tab_spec,
                cs(wkv.shape), cs(gkva.shape), cs(wmisc.shape), cs(gkr.shape), tab_spec, tab_spec, tab_spec,
                cs(wf.shape), cs(gfq.shape), cs(gfk.shape), cs(ef.shape), cs(bfv.shape)]
    kv_lora = wkv.shape[1]
    fw = wf.shape[1] // 3
    n_fh = fw // FOX_DIM
    qw = wqb.shape[1]

    def sds(width, dtype):
        return jax.ShapeDtypeStruct((nseq, slen, width), dtype)

    out_shape = [sds(kv_lora, F32), sds(MLA_ROPE, F32), sds(fw, F32), sds(fw, F32), sds(n_fh, F32),
                 sds(qw, BF16)]
    out_specs = [tok(kv_lora), tok(MLA_ROPE), tok(fw), tok(fw), tok(n_fh), tok(qw)]
    scratch = []
    if prompt:
        assert sb == 1
        tril = jnp.asarray(np.tril(np.ones((tb, tb), np.float32)), BF16)
        pc = list(prompt_consts[:8]) + [tril] + list(prompt_consts[8:])
        args += pc
        in_specs += [cs(a.shape) for a in pc]
        vrows = prompt_consts[4].shape[0]
        vt_spec = pl.BlockSpec((1, vrows, tb), lambda i, j: (i, 0, j))
        vt_sds = jax.ShapeDtypeStruct((nseq, vrows, slen), BF16)
        out_shape += [sds(qw, BF16), vt_sds, sds(n_fh * LANES, BF16), sds(n_fh * LANES, BF16), vt_sds]
        out_specs += [tok(qw), vt_spec, tok(n_fh * LANES), tok(n_fh * LANES), vt_spec]
        scratch = [pltpu.VMEM((1, LANES), F32)]
    else:
        out_shape += [sds(fw, BF16)]
        out_specs += [tok(fw)]
    return pl.pallas_call(
        functools.partial(_token_kernel, prompt=prompt),
        grid=grid, in_specs=in_specs, out_specs=out_specs, out_shape=out_shape,
        scratch_shapes=scratch,
        compiler_params=pltpu.CompilerParams(
            dimension_semantics=("parallel", "arbitrary"), vmem_limit_bytes=VMEM_LIMIT),
        name="token_prompt" if prompt else "token_sample",
    )(*args)


def _cumsum_pages_kernel(lt_ref, triu_ref, o_ref):
    r, nh, p = lt_ref.shape
    triu = triu_ref[...]
    hi, mid, lo = _split3(lt_ref[...].reshape(r * nh, p))
    c = (jnp.dot(hi, triu, preferred_element_type=F32)
         + jnp.dot(mid, triu, preferred_element_type=F32)
         + jnp.dot(lo, triu, preferred_element_type=F32))
    o_ref[...] = c.reshape(r, nh, p)


def _cumsum_pages_call(pool_t, rows):
    n_phys, nh, p = pool_t.shape
    triu = jnp.asarray(np.triu(np.ones((p, p), np.float32)), BF16)
    return pl.pallas_call(
        _cumsum_pages_kernel,
        grid=(n_phys // rows,),
        in_specs=[pl.BlockSpec((rows, nh, p), lambda i: (i, 0, 0)), _const_spec((p, p))],
        out_specs=pl.BlockSpec((rows, nh, p), lambda i: (i, 0, 0)),
        out_shape=jax.ShapeDtypeStruct((n_phys, nh, p), F32),
        compiler_params=pltpu.CompilerParams(dimension_semantics=("parallel",)),
        name="cumsum_pages",
    )(pool_t, triu)


def _flash_kernel(qi_ref, ki_ref, q_ref, k_ref, vt_ref, o_ref, m_sc, acc_sc):
    qi = qi_ref[pl.program_id(2)]
    ki = ki_ref[pl.program_id(2)]
    tq = q_ref.shape[1]
    tk = k_ref.shape[1]
    cq = min(FLASH_CQ, tq)
    ck = min(FLASH_CK, tk)

    @pl.when(ki == 0)
    def _():
        m_sc[...] = jnp.full_like(m_sc, NEG_INF)
        acc_sc[...] = jnp.zeros_like(acc_sc)

    def step(masked):
        for hh in range(2):
            for q0 in range(0, tq, cq):
                q = q_ref[0, q0:q0 + cq, hh * LANES:(hh + 1) * LANES]
                m = m_sc[hh, :, q0:q0 + cq]
                acc = acc_sc[hh, :, q0:q0 + cq]
                for k0 in range(0, tk, ck):
                    if masked and k0 > q0 + cq - 1:
                        continue
                    k = k_ref[0, k0:k0 + ck, hh * LANES:(hh + 1) * LANES]
                    st = _dot_nt(k, q)
                    if masked and k0 + ck - 1 > q0:
                        key = k0 + lax.broadcasted_iota(jnp.int32, (ck, cq), 0)
                        qry = q0 + lax.broadcasted_iota(jnp.int32, (ck, cq), 1)
                        st = jnp.where(key <= qry, st, NEG_INF)
                    st3 = st.reshape(ck // SUBLANES, SUBLANES, cq)
                    m_cur = jnp.max(jnp.max(st3, axis=0), axis=0, keepdims=True)
                    m_new = jnp.maximum(m, m_cur)
                    alpha = jnp.exp2(m - m_new)
                    p = jnp.exp2(st3 - m_new[None]).reshape(ck, cq)
                    vt = vt_ref[0, hh * V_ROWS:(hh + 1) * V_ROWS, k0:k0 + ck]
                    acc = (acc.reshape(V_ROWS // SUBLANES, SUBLANES, cq) * alpha[None]).reshape(V_ROWS, cq)
                    acc = acc + _dot(vt, p)
                    m = m_new
                m_sc[hh, :, q0:q0 + cq] = m
                acc_sc[hh, :, q0:q0 + cq] = acc

    @pl.when(ki < qi)
    def _():
        step(False)

    @pl.when(ki == qi)
    def _():
        step(True)
        outs = []
        for hh in range(2):
            acc = acc_sc[hh]
            outs.append(acc[:MLA_V, :] / acc[MLA_V:MLA_V + 1, :])
        o_ref[0] = jnp.concatenate(outs, axis=0).T.astype(o_ref.dtype)


def _flash_call(q, k, vt, *, tile, name):
    b, s, qw_all = q.shape
    hp = vt.shape[1] // (2 * V_ROWS)
    qw = qw_all // hp
    assert qw == 2 * LANES
    n_t = s // tile
    pairs = [(qi, ki) for qi in range(n_t) for ki in range(qi + 1)]
    qi_tab = jnp.asarray([p[0] for p in pairs], jnp.int32)
    ki_tab = jnp.asarray([p[1] for p in pairs], jnp.int32)
    in_specs = [pl.BlockSpec((1, tile, qw), lambda bi, h, p, qt, kt: (bi, qt[p], h)),
                pl.BlockSpec((1, tile, qw), lambda bi, h, p, qt, kt: (bi, kt[p], h)),
                pl.BlockSpec((1, 2 * V_ROWS, tile), lambda bi, h, p, qt, kt: (bi, h, kt[p]))]
    grid_spec = pltpu.PrefetchScalarGridSpec(
        num_scalar_prefetch=2, grid=(b, hp, len(pairs)), in_specs=in_specs,
        out_specs=pl.BlockSpec((1, tile, LANES), lambda bi, h, p, qt, kt: (bi, qt[p], h)),
        scratch_shapes=[pltpu.VMEM((2, SUBLANES, tile), F32), pltpu.VMEM((2, V_ROWS, tile), F32)])
    return pl.pallas_call(
        _flash_kernel,
        grid_spec=grid_spec,
        out_shape=jax.ShapeDtypeStruct((b, s, hp * LANES), BF16),
        compiler_params=pltpu.CompilerParams(
            dimension_semantics=("parallel", "parallel", "arbitrary"),
            vmem_limit_bytes=VMEM_LIMIT),
        name=name,
    )(qi_tab, ki_tab, q, k, vt)


def _head_rows(x, n_heads, width):
    lane = lax.broadcasted_iota(jnp.int32, x.shape, 1)
    return jnp.concatenate(
        [jnp.where((lane >= h * width) & (lane < (h + 1) * width), x, jnp.zeros_like(x))
         for h in range(n_heads)], axis=0)


def _head_diag(full, n_heads, width, t):
    lane = lax.broadcasted_iota(jnp.int32, (t, full.shape[1]), 1)
    out = jnp.zeros((t, full.shape[1]), full.dtype)
    for h in range(n_heads):
        blk = full[h * t:(h + 1) * t, :]
        out = out + jnp.where((lane >= h * width) & (lane < (h + 1) * width), blk, 0.0)
    return out


def _online_update(s, pv, m_sc, l_sc, acc_sc, first):
    if first:
        m_new = jnp.max(s, axis=1, keepdims=True)
        p = jnp.exp2(s - m_new)
        l_sc[...] = jnp.sum(p, axis=1, keepdims=True)
        acc_sc[...] = pv(p)
    else:
        m_prev = m_sc[...]
        m_new = jnp.maximum(m_prev, jnp.max(s, axis=1, keepdims=True))
        alpha = jnp.exp2(m_prev - m_new)
        p = jnp.exp2(s - m_new)
        l_sc[...] = alpha * l_sc[...] + jnp.sum(p, axis=1, keepdims=True)
        acc_sc[...] = alpha * acc_sc[...] + pv(p)
    m_sc[...] = m_new


def _new_key_mask(rows, t):
    row = lax.broadcasted_iota(jnp.int32, (rows, PAGE), 0)
    col = lax.broadcasted_iota(jnp.int32, (rows, PAGE), 1)
    return col <= row % t


def _pad_rows(x, rows):
    return jnp.concatenate([x, jnp.zeros((rows - x.shape[0], x.shape[1]), x.dtype)], axis=0)


def _mla_decode_kernel(pt_ref, q_ref, cnew_ref, krnew_ref, sel_ref, gkn_ref, wkn_ref, wv_ref, es_ref,
                       *rest, n_pp, n_heads):
    ckv_refs = rest[:n_pp]
    krt_refs = rest[n_pp:2 * n_pp]
    o_ref, qbd_sc, qr_sc, m_sc, l_sc, acc_sc = rest[2 * n_pp:]
    j = pl.program_id(1)
    t = q_ref.shape[1]

    def nope_scores(ckv_b):
        kraw = _dot(ckv_b, wkn_ref[...])
        s = _dot_nt(qbd_sc[...], kraw)
        ssq = _dot_nt(es_ref[...], kraw * kraw)
        return s * lax.rsqrt(ssq + EPS)

    @pl.when(j == 0)
    def _():
        qm = q_ref[0].astype(F32)
        qn = _dot(qm, sel_ref[...]) * gkn_ref[...]
        qbd_sc[...] = _head_rows(qn, n_heads, MLA_NOPE).astype(BF16)
        qr_sc[...] = jnp.concatenate(
            [qm[:, h * LANES + MLA_NOPE:h * LANES + MLA_NOPE + MLA_ROPE] for h in range(n_heads)],
            axis=0).astype(BF16)
        ckv_b = _pad_rows(cnew_ref[0], PAGE).astype(BF16)
        s = nope_scores(ckv_b) + _dot_nt(qr_sc[...], _pad_rows(krnew_ref[0], PAGE))
        s = jnp.where(_new_key_mask(n_heads * t, t), s, NEG_INF)
        _online_update(s, lambda p: _dot(p, ckv_b), m_sc, l_sc, acc_sc, first=True)

    ckv_b = jnp.concatenate([r[0].astype(BF16) for r in ckv_refs], axis=0)
    krt = jnp.concatenate([r[0].astype(BF16) for r in krt_refs], axis=1)
    s = nope_scores(ckv_b) + _dot(qr_sc[...], krt)
    _online_update(s, lambda p: _dot(p, ckv_b), m_sc, l_sc, acc_sc, first=False)

    @pl.when(j == pl.num_programs(1) - 1)
    def _():
        o_lat = acc_sc[...] / l_sc[...]
        full = _dot(o_lat, wv_ref[...])
        o_ref[0] = _head_diag(full, n_heads, MLA_V, t).astype(o_ref.dtype)


def _page_map(slot, n_pp, n_pages, nd):
    def index_map(b, j, pt):
        logical = n_pages - 1 - (j * n_pp + slot)
        return (pt[b * n_pages + logical],) + (0,) * (nd - 1)
    return index_map


def _fox_decode_kernel(pt_ref, q_ref, knew_ref, vnew_ref, lnew_ref, lnewt_ref, *rest, n_pp, n_heads):
    kt_refs = rest[:n_pp]
    vt_refs = rest[n_pp:2 * n_pp]
    c_refs = rest[2 * n_pp:3 * n_pp]
    o_ref, qbd_sc, acol_sc, suf_sc, m_sc, l_sc, acc_sc = rest[3 * n_pp:]
    j = pl.program_id(1)
    t = q_ref.shape[1]

    def head_rows_bias(x):
        return jnp.concatenate(
            [jnp.broadcast_to(x[h:h + 1, :], (t, x.shape[1])) for h in range(n_heads)], axis=0)

    @pl.when(j == 0)
    def _():
        qbd_sc[...] = _head_rows(q_ref[0].astype(F32), n_heads, FOX_DIM).astype(BF16)
        lnew = lnew_ref[0] * LOG2E
        row = lax.broadcasted_iota(jnp.int32, lnew.shape, 0)
        cnew = jnp.zeros_like(lnew)
        for tt in range(t):
            cnew = cnew + jnp.where(row >= tt, lnew[tt:tt + 1, :], 0.0)
        for h in range(n_heads):
            acol_sc[h * t:(h + 1) * t, :] = cnew[:, h:h + 1]
        lnt = lnewt_ref[0] * LOG2E
        lane = lax.broadcasted_iota(jnp.int32, lnt.shape, 1)
        cnew_t = jnp.zeros_like(lnt)
        for tt in range(t):
            cnew_t = cnew_t + jnp.where(lane >= tt, lnt[:, tt:tt + 1], 0.0)
        suf_sc[...] = jnp.zeros_like(suf_sc)
        k_b = _pad_rows(knew_ref[0], PAGE).astype(BF16)
        v_b = _pad_rows(vnew_ref[0], PAGE).astype(BF16)
        s = _dot_nt(qbd_sc[...], k_b) + (acol_sc[...] - head_rows_bias(cnew_t))
        s = jnp.where(_new_key_mask(n_heads * t, t), s, NEG_INF)
        _online_update(s, lambda p: _dot(p, v_b), m_sc, l_sc, acc_sc, first=True)

    suffix = suf_sc[...]
    biases = []
    for p in range(n_pp):
        c = c_refs[p][0] * LOG2E
        total = c[:, PAGE - 1:PAGE]
        biases.append(suffix + (total - c))
        suffix = suffix + total
    suf_sc[...] = suffix
    bias = biases[0] if n_pp == 1 else jnp.concatenate(biases, axis=1)
    kt_b = jnp.concatenate([r[0].astype(BF16) for r in kt_refs], axis=1)
    vt_b = jnp.concatenate([r[0].astype(BF16) for r in vt_refs], axis=1)
    s = _dot(qbd_sc[...], kt_b) + (acol_sc[...] + head_rows_bias(bias))
    _online_update(s, lambda p: _dot_nt(p, vt_b), m_sc, l_sc, acc_sc, first=False)

    @pl.when(j == pl.num_programs(1) - 1)
    def _():
        o_ref[0] = _head_diag(acc_sc[...] / l_sc[...], n_heads, FOX_DIM, t).astype(o_ref.dtype)


N_MLA_FIXED = 8
N_FOX_FIXED = 5
N_MLA_SCRATCH = 5


def _decode_kernel(pt_ref, *refs, n_pp, n_heads):
    n_mla = N_MLA_FIXED + 2 * n_pp
    n_fox = N_FOX_FIXED + 3 * n_pp
    mla_in, fox_in = refs[:n_mla], refs[n_mla:n_mla + n_fox]
    o_mla, o_fox = refs[n_mla + n_fox:n_mla + n_fox + 2]
    scratch = refs[n_mla + n_fox + 2:]
    _mla_decode_kernel(pt_ref, *mla_in, o_mla, *scratch[:N_MLA_SCRATCH], n_pp=n_pp, n_heads=n_heads)
    _fox_decode_kernel(pt_ref, *fox_in, o_fox, *scratch[N_MLA_SCRATCH:], n_pp=n_pp, n_heads=n_heads)


def _decode_call(pt_flat, mla_args, mla_pools, fox_args, fox_pools, *, n_pp):
    q, ckv_new, kr_new, sel, gkn, wkn, wv, es = mla_args
    pool_ckv, pool_krt = mla_pools
    fq, k_new, v_new, l_new, l_new_t = fox_args
    pool_kt, pool_vt, pool_c = fox_pools
    bd, t, qw = q.shape
    fw = fq.shape[2]
    n_pages = pt_flat.shape[0] // bd
    n_heads = qw // LANES
    assert n_heads == fw // FOX_DIM
    kv_lora = pool_ckv.shape[-1]
    ow = wv.shape[1]
    rows = n_heads * t

    def batch(shape):
        return pl.BlockSpec((1,) + shape, lambda b, j, pt: (b, 0, 0))

    def pages(shape):
        return [pl.BlockSpec((1,) + shape, _page_map(p, n_pp, n_pages, 3)) for p in range(n_pp)]

    cs = _const_spec
    in_specs = [batch((t, qw)), batch((t, kv_lora)), batch((t, MLA_ROPE)),
                cs(sel.shape), cs(gkn.shape), cs(wkn.shape), cs(wv.shape), cs(es.shape)]
    in_specs += pages((PAGE, kv_lora)) + pages((MLA_ROPE, PAGE))
    in_specs += [batch((t, fw)), batch((t, fw)), batch((t, fw)), batch((t, n_heads)), batch((n_heads, PAGE))]
    in_specs += pages((fw, PAGE)) + pages((fw, PAGE)) + pages((n_heads, PAGE))
    assert len(mla_args) == N_MLA_FIXED and len(fox_args) == N_FOX_FIXED
    grid_spec = pltpu.PrefetchScalarGridSpec(
        num_scalar_prefetch=1, grid=(bd, n_pages // n_pp), in_specs=in_specs,
        out_specs=[pl.BlockSpec((1, t, ow), lambda b, j, pt: (b, 0, 0)),
                   pl.BlockSpec((1, t, fw), lambda b, j, pt: (b, 0, 0))],
        scratch_shapes=[pltpu.VMEM((rows, wkn.shape[1]), BF16), pltpu.VMEM((rows, MLA_ROPE), BF16),
                        pltpu.VMEM((rows, 1), F32), pltpu.VMEM((rows, 1), F32),
                        pltpu.VMEM((rows, kv_lora), F32),
                        pltpu.VMEM((rows, fw), BF16), pltpu.VMEM((rows, 1), F32),
                        pltpu.VMEM((n_heads, 1), F32),
                        pltpu.VMEM((rows, 1), F32), pltpu.VMEM((rows, 1), F32),
                        pltpu.VMEM((rows, fw), F32)])
    return pl.pallas_call(
        functools.partial(_decode_kernel, n_pp=n_pp, n_heads=n_heads),
        grid_spec=grid_spec,
        out_shape=[jax.ShapeDtypeStruct((bd, t, ow), BF16), jax.ShapeDtypeStruct((bd, t, fw), BF16)],
        compiler_params=pltpu.CompilerParams(
            dimension_semantics=("parallel", "arbitrary"), vmem_limit_bytes=VMEM_LIMIT),
        name="paged_decode",
    )(pt_flat, *mla_args, *([pool_ckv] * n_pp), *([pool_krt] * n_pp),
      *fox_args, *([pool_kt] * n_pp), *([pool_vt] * n_pp), *([pool_c] * n_pp))


def _out_mlp_kernel(x_ref, om_ref, of_ref, ga_ref, shm_ref, scm_ref, gm_ref, gmlp_ref,
                    wom_ref, wof_ref, w1_ref, w2_ref, y_ref, *, ff_chunk):
    sb, tb, d = x_ref.shape
    tm = sb * tb
    o = (_dot(om_ref[...].reshape(tm, om_ref.shape[-1]), wom_ref[...])
         + _dot(of_ref[...].reshape(tm, of_ref.shape[-1]), wof_ref[...]))
    x1 = x_ref[...] + ga_ref[...] * o.reshape(sb, tb, d)
    ms = jnp.mean(x1 * x1, axis=-1, keepdims=True)
    h = x1 * lax.rsqrt(ms + EPS) * gmlp_ref[...]
    h = h * (1.0 + scm_ref[...]) + shm_ref[...]
    hb = h.reshape(tm, d).astype(BF16)
    d_ff = w1_ref.shape[1]
    acc = jnp.zeros((tm, d), F32)
    for c in range(d_ff // ff_chunk):
        u = jnp.maximum(_dot(hb, w1_ref[:, c * ff_chunk:(c + 1) * ff_chunk]), 0.0)
        acc = acc + _dot(u * u, w2_ref[c * ff_chunk:(c + 1) * ff_chunk, :])
    y_ref[...] = x1 + gm_ref[...] * acc.reshape(sb, tb, d)


def _out_mlp_call(x, o_mla, o_fox, gate_a, shift_m, scale_m, gate_m, g_mlp, wom, wof, w1, w2, *, sb, tb):
    nseq, slen, d = x.shape

    def tok(width):
        return pl.BlockSpec((sb, tb, width), lambda i, j: (i, j, 0))

    mod_spec = pl.BlockSpec((sb, 1, d), lambda i, j: (i, 0, 0))

    def weight(shape):
        return pl.BlockSpec(shape, lambda i, j: (0, 0), pipeline_mode=pl.Buffered(1))

    return pl.pallas_call(
        functools.partial(_out_mlp_kernel, ff_chunk=min(1024, w1.shape[1])),
        grid=(nseq // sb, slen // tb),
        in_specs=[tok(d), tok(o_mla.shape[-1]), tok(o_fox.shape[-1]),
                  mod_spec, mod_spec, mod_spec, mod_spec, _const_spec(g_mlp.shape),
                  weight(wom.shape), weight(wof.shape), weight(w1.shape), weight(w2.shape)],
        out_specs=tok(d),
        out_shape=jax.ShapeDtypeStruct((nseq, slen, d), F32),
        compiler_params=pltpu.CompilerParams(
            dimension_semantics=("parallel", "parallel"), vmem_limit_bytes=VMEM_LIMIT),
        name="out_mlp",
    )(x, o_mla, o_fox, gate_a, shift_m, scale_m, gate_m, g_mlp, wom, wof, w1, w2)


def _rope_tables(pos, scale_q):
    half = MLA_ROPE // 2
    inv_freq = ROPE_THETA ** (-jnp.arange(half, dtype=F32) / half)
    ang = pos.astype(F32)[:, None] * inv_freq[None, :]
    cos, sin = jnp.cos(ang), jnp.sin(ang)
    n = pos.shape[0]
    z = lambda w: jnp.zeros((n, w), F32)
    one = jnp.ones((n, MLA_NOPE), F32)
    tqc = jnp.concatenate([one, cos, cos, z(LANES - MLA_NOPE - MLA_ROPE)], axis=1) * scale_q
    tqs1 = jnp.concatenate([z(MLA_NOPE), -sin, z(LANES - MLA_NOPE - half)], axis=1) * scale_q
    tqs2 = jnp.concatenate([z(MLA_NOPE + half), sin, z(LANES - MLA_NOPE - MLA_ROPE)], axis=1) * scale_q
    tkc = jnp.concatenate([cos, cos, z(LANES - MLA_ROPE)], axis=1)
    tks1 = jnp.concatenate([-sin, z(LANES - half)], axis=1)
    tks2 = jnp.concatenate([z(half), sin, z(LANES - MLA_ROPE)], axis=1)
    return tuple(t[None] for t in (tqc, tqs1, tqs2, tkc, tks1, tks2))


def _layer_consts(w_in, b_f, g_attn, g_qa, w_qb, g_qn, g_qr, g_kva, g_kr, w_kvb, g_kn, g_fq, g_fk):
    d = w_in.shape[0]
    q_lora, n_mh, _ = w_qb.shape
    kv_lora = w_kvb.shape[0]
    n_fh = b_f.shape[0]
    fw = n_fh * FOX_DIM
    c0, c1, c2 = q_lora, q_lora + kv_lora, q_lora + kv_lora + MLA_ROPE
    c3 = c2 + 3 * fw
    pad_h = LANES - MLA_NOPE - MLA_ROPE
    row = lambda v: v.reshape(1, -1).astype(F32)

    wq = w_in[:, :c0].astype(BF16)
    wkv = w_in[:, c0:c1].astype(BF16)
    wmisc = jnp.concatenate([w_in[:, c1:c2], w_in[:, c3:c3 + n_fh],
                             jnp.zeros((d, LANES - MLA_ROPE - n_fh), F32)], axis=1).astype(BF16)
    wf = w_in[:, c2:c3].astype(BF16)
    wqb = jnp.pad(w_qb, ((0, 0), (0, 0), (0, pad_h))).reshape(q_lora, n_mh * LANES).astype(BF16)
    gq = row(jnp.concatenate([g_qn, g_qr, jnp.zeros((n_mh, pad_h), F32)], axis=1))
    wkn = jnp.pad(w_kvb[:, :, :MLA_NOPE], ((0, 0), (0, 0), (0, LANES - MLA_NOPE)))
    wkn = wkn.reshape(kv_lora, n_mh * LANES).astype(BF16)
    gk = row(jnp.pad(g_kn, ((0, 0), (0, LANES - MLA_NOPE))))
    wkn_c = w_kvb[:, :, :MLA_NOPE].reshape(kv_lora, n_mh * MLA_NOPE).astype(BF16)
    wv_c = w_kvb[:, :, MLA_NOPE:].reshape(kv_lora, n_mh * MLA_V).astype(BF16)
    gkr = row(jnp.pad(g_kr, (0, LANES - MLA_ROPE)))
    bfv = row(jnp.concatenate([jnp.zeros((LOGF_LANE,), F32), b_f, jnp.zeros((LANES - LOGF_LANE - n_fh,), F32)]))
    gfq = row(g_fq) * (FOX_DIM ** -0.5 * LOG2E)
    gfk = row(g_fk)

    pad_v = V_ROWS - MLA_V
    wvt = jnp.pad(jnp.transpose(w_kvb[:, :, MLA_NOPE:], (1, 2, 0)), ((0, 0), (0, pad_v), (0, 0)))
    wvt = wvt.reshape(n_mh * V_ROWS, kv_lora).astype(BF16)
    wfv = w_in[:, c2 + 2 * fw:c3].reshape(d, n_fh, FOX_DIM)
    wfvt = jnp.pad(jnp.transpose(wfv, (1, 2, 0)), ((0, 0), (0, pad_v), (0, 0)))
    wfvt = wfvt.reshape(n_fh * V_ROWS, d).astype(BF16)
    assert n_mh == n_fh
    ones_col = np.zeros((n_mh, V_ROWS, 1), np.float32)
    ones_col[:, MLA_V:, :] = 1.0
    ones_col = jnp.asarray(ones_col.reshape(n_mh * V_ROWS, 1))

    idx = np.arange(LANES)
    nope = idx < MLA_NOPE
    rope = (idx >= MLA_NOPE) & (idx < MLA_NOPE + MLA_ROPE)
    eq = (np.outer(nope, nope) / MLA_NOPE + np.outer(rope, rope) / MLA_ROPE).astype(np.float32)
    ek = (np.outer(nope, nope) / MLA_NOPE).astype(np.float32)
    ef = ((idx[:, None] // FOX_DIM) == (idx[None, :] // FOX_DIM)).astype(np.float32) / FOX_DIM
    rep = np.zeros((LANES, n_mh * LANES), np.float32)
    sel = np.zeros((n_mh * LANES, n_mh * MLA_NOPE), np.float32)
    placef = np.zeros((fw, n_fh * LANES), np.float32)
    placeq = np.zeros((LANES, n_fh * LANES), np.float32)
    placek = np.zeros((LANES, n_fh * LANES), np.float32)
    constq = np.zeros((1, n_fh * LANES), np.float32)
    constk = np.zeros((1, n_fh * LANES), np.float32)
    for h in range(n_mh):
        rep[np.arange(MLA_ROPE), h * LANES + MLA_NOPE + np.arange(MLA_ROPE)] = 1.0
        sel[h * LANES + np.arange(MLA_NOPE), h * MLA_NOPE + np.arange(MLA_NOPE)] = 1.0
    for h in range(n_fh):
        placef[h * FOX_DIM + np.arange(FOX_DIM), h * LANES + np.arange(FOX_DIM)] = 1.0
        for part in range(3):
            placeq[LOGF_LANE + part * n_fh + h, h * LANES + BIAS_LANE + part] = 1.0
            placek[LOGF_LANE + part * n_fh + h, h * LANES + BIAS_LANE + 3 + part] = -1.0
            constq[0, h * LANES + BIAS_LANE + 3 + part] = 1.0
            constk[0, h * LANES + BIAS_LANE + part] = 1.0
    t_dec = SUBLANES
    es = np.zeros((n_mh * t_dec, n_mh * MLA_NOPE), np.float32)
    for h in range(n_mh):
        es[h * t_dec:(h + 1) * t_dec, h * MLA_NOPE:(h + 1) * MLA_NOPE] = 1.0 / MLA_NOPE

    bf = lambda a: jnp.asarray(a, BF16)
    token_consts = (row(g_attn), wq, row(g_qa), wqb, gq, bf(eq), wkv, row(g_kva),
                    wmisc, gkr, wf, gfq, gfk, bf(ef), bfv)
    prompt_consts = (wkn, gk, bf(ek), bf(rep), wvt, wfvt, ones_col, bf(placef),
                     bf(placeq), bf(placek), jnp.asarray(constq), jnp.asarray(constk))
    decode_consts = (bf(sel), row(g_kn), wkn_c, wv_c, bf(es))
    return token_consts, prompt_consts, decode_consts


def kernel(x_prompt, x_sample, cache_mla_ckv, cache_mla_krope, cache_fox_k, cache_fox_v, cache_fox_logf,
           page_table, c_prompt, c_sample, w_ada, b_ada, g_attn, g_mlp, w_in, b_f, g_qa, w_qb, g_qn,
           g_qr, g_kva, g_kr, w_kvb, g_kn, g_fq, g_fk, w_o, w_mlp1, w_mlp2):
    depth = w_ada.shape[0]
    b, s, d = x_prompt.shape
    bd, t_dec, _ = x_sample.shape
    assert t_dec == SUBLANES, "decode kernels lay the new tokens on one sublane tile"
    n_pages = page_table.shape[1]
    past_len = n_pages * PAGE
    n_phys = cache_mla_ckv.shape[1]
    n_fh = b_f.shape[1]
    fw = n_fh * FOX_DIM
    n_mh = w_qb.shape[2]
    assert n_fh % 2 == 0 and n_mh % 2 == 0
    assert 3 * n_fh <= LANES - LOGF_LANE

    tile_p = min(512, s)
    sb_s = min(64, bd)
    n_pp = min(16, n_pages)
    cum_rows = math.gcd(n_phys, 256)
    assert s % tile_p == 0 and bd % sb_s == 0 and n_pages % n_pp == 0

    scale_mla = (MLA_NOPE + MLA_ROPE) ** -0.5 * LOG2E
    tabs_p = _rope_tables(jnp.arange(s, dtype=jnp.int32), scale_mla)
    tabs_s = _rope_tables(past_len + jnp.arange(t_dec, dtype=jnp.int32), scale_mla)
    pt_flat = page_table.reshape(-1).astype(jnp.int32)

    y_p, y_s = x_prompt, x_sample
    rows_p, rows_s = [], []
    for l in range(depth):
        token_consts, prompt_consts, decode_consts = _layer_consts(
            w_in[l], b_f[l], g_attn[l], g_qa[l], w_qb[l], g_qn[l], g_qr[l], g_kva[l], g_kr[l],
            w_kvb[l], g_kn[l], g_fq[l], g_fk[l])
        mod = _ada_call(jnp.concatenate([c_prompt, c_sample], axis=0), w_ada[l], b_ada[l])
        mods_p = [m[:, None, :] for m in jnp.split(mod[:b], 6, axis=-1)]
        mods_s = [m[:, None, :] for m in jnp.split(mod[b:], 6, axis=-1)]

        (p_ckv, p_krope, p_k, p_v, p_logf, q_mla, k_mla, vt_mla, fq_aug, fk_aug, vt_fox) = _token_call(
            y_p, mods_p[0], mods_p[1], token_consts, prompt_consts, tabs_p, prompt=True, sb=1, tb=tile_p)
        (s_ckv, s_krope, s_k, s_v, s_logf, q_mla_s, fq_s) = _token_call(
            y_s, mods_s[0], mods_s[1], token_consts, None, tabs_s, prompt=False, sb=sb_s, tb=t_dec)

        o_mla_p = _flash_call(q_mla, k_mla, vt_mla, tile=tile_p, name="mla_prompt")
        o_fox_p = _flash_call(fq_aug, fk_aug, vt_fox, tile=tile_p, name="fox_prompt")

        sel, gkn, wkn_c, wv_c, es = decode_consts
        pool_krt = jnp.swapaxes(cache_mla_krope[l], 1, 2)
        pool_kt = jnp.transpose(cache_fox_k[l], (0, 2, 3, 1)).reshape(n_phys, fw, PAGE)
        pool_vt = jnp.transpose(cache_fox_v[l], (0, 2, 3, 1)).reshape(n_phys, fw, PAGE)
        pool_c = _cumsum_pages_call(jnp.swapaxes(cache_fox_logf[l], 1, 2), cum_rows)
        l_new_t = jnp.pad(jnp.swapaxes(s_logf, 1, 2), ((0, 0), (0, 0), (0, PAGE - t_dec)))
        o_mla_s, o_fox_s = _decode_call(
            pt_flat, (q_mla_s, s_ckv, s_krope, sel, gkn, wkn_c, wv_c, es), (cache_mla_ckv[l], pool_krt),
            (fq_s, s_k, s_v, s_logf, l_new_t), (pool_kt, pool_vt, pool_c), n_pp=n_pp)

        wo = w_o[l].astype(BF16)
        wom, wof = wo[:n_mh * MLA_V], wo[n_mh * MLA_V:]
        w1 = w_mlp1[l].astype(BF16)
        w2 = w_mlp2[l].astype(BF16)
        g_mlp_row = g_mlp[l].reshape(1, -1)
        y_p = _out_mlp_call(y_p, o_mla_p, o_fox_p, mods_p[2], mods_p[3], mods_p[4], mods_p[5],
                            g_mlp_row, wom, wof, w1, w2, sb=1, tb=tile_p)
        y_s = _out_mlp_call(y_s, o_mla_s, o_fox_s, mods_s[2], mods_s[3], mods_s[4], mods_s[5],
                            g_mlp_row, wom, wof, w1, w2, sb=sb_s, tb=t_dec)

        rows_p.append((p_ckv, p_krope, p_k.reshape(b, s, n_fh, FOX_DIM), p_v.reshape(b, s, n_fh, FOX_DIM), p_logf))
        rows_s.append((s_ckv, s_krope, s_k.reshape(bd, t_dec, n_fh, FOX_DIM),
                       s_v.reshape(bd, t_dec, n_fh, FOX_DIM), s_logf))

    outs_p = [jnp.stack(t, axis=0) for t in zip(*rows_p)]
    outs_s = [jnp.stack(t, axis=0) for t in zip(*rows_s)]
    return (y_p, y_s, *outs_p, *outs_s)
```

```python
import functools
import math

import jax
import jax.numpy as jnp
import numpy as np
from jax import lax
from jax.experimental import pallas as pl
from jax.experimental.pallas import tpu as pltpu

F32 = jnp.float32
BF16 = jnp.bfloat16

LANES = 128
SUBLANES = 8
BF16_ROWS = 16
VMEM_LIMIT = 56 * 1024 * 1024

MLA_NOPE = 64
MLA_ROPE = 32
MLA_V = 64
FOX_DIM = 64
PAGE = 128
ROPE_THETA = 10000.0
EPS = 1e-6
NEG_INF = -1e30
LOG2E = math.log2(math.e)

V_ROWS = MLA_V + BF16_ROWS
FLASH_CQ = 256
BIAS_LANE = FOX_DIM
LOGF_LANE = MLA_ROPE


def _dot(a, b):
    return jnp.dot(a.astype(BF16), b.astype(BF16), preferred_element_type=F32)


def _dot_nt(a, b):
    return lax.dot_general(a.astype(BF16), b.astype(BF16), (((1,), (1,)), ((), ())),
                           preferred_element_type=F32)


def _split3(x):
    hi = x.astype(BF16)
    r1 = x - hi.astype(F32)
    mid = r1.astype(BF16)
    lo = (r1 - mid.astype(F32)).astype(BF16)
    return hi, mid, lo


def _const_spec(shape):
    nd = len(shape)
    return pl.BlockSpec(shape, lambda *_, _nd=nd: (0,) * _nd)


def _group_mean(sq, e_ref):
    n = sq.shape[1] // LANES
    e = e_ref[...]
    parts = [_dot(sq[:, g * LANES:(g + 1) * LANES], e) for g in range(n)]
    return parts[0] if n == 1 else jnp.concatenate(parts, axis=1)


def _log_sigmoid(x):
    return jnp.minimum(x, 0.0) - jnp.log1p(jnp.exp(-jnp.abs(x)))


def _ada_kernel(c_ref, w_ref, b_ref, o_ref):
    c = c_ref[...]
    o_ref[...] = _dot(c * jax.nn.sigmoid(c), w_ref[...]) + b_ref[...]


def _ada_call(c, w_ada, b_ada):
    rows, d = c.shape
    n_chunks = w_ada.shape[1] // d
    return pl.pallas_call(
        _ada_kernel,
        grid=(n_chunks,),
        in_specs=[pl.BlockSpec((rows, d), lambda i: (0, 0)),
                  pl.BlockSpec((d, d), lambda i: (0, i)),
                  pl.BlockSpec((1, d), lambda i: (0, i))],
        out_specs=pl.BlockSpec((rows, d), lambda i: (0, i)),
        out_shape=jax.ShapeDtypeStruct((rows, n_chunks * d), F32),
        compiler_params=pltpu.CompilerParams(dimension_semantics=("arbitrary",)),
        name="adaln_mod",
    )(c, w_ada, b_ada.reshape(1, -1))


def _token_kernel(x_ref, sh_ref, sc_ref, gattn_ref,
                  wq_ref, gqa_ref, wqb_ref, gq_ref, eq_ref, tqc_ref, tqs1_ref, tqs2_ref,
                  wkv_ref, gkva_ref, wmisc_ref, gkr_ref, tkc_ref, tks1_ref, tks2_ref,
                  wf_ref, gfq_ref, gfk_ref, ef_ref, bf_ref,
                  *rest, prompt):
    if prompt:
        (wkn_ref, gk_ref, ek_ref, rep_ref, wvt_ref, wfvt_ref, ones_ref, placef_ref,
         tril_ref, placeq_ref, placek_ref, constq_ref, constk_ref,
         ckv_o, krope_o, fk_o, fv_o, logf_o, q_o,
         kmla_o, vtm_o, fqa_o, fka_o, vtf_o, carry_sc) = rest
    else:
        (ckv_o, krope_o, fk_o, fv_o, logf_o, q_o, fq_o) = rest
    sb, tb, d = x_ref.shape
    tm = sb * tb

    def to3(a):
        return a.reshape(sb, tb, a.shape[-1])

    def tiled(t_ref, width):
        t = t_ref[...]
        reps = width // t.shape[-1]
        return t if reps == 1 else jnp.concatenate([t] * reps, axis=-1)

    x = x_ref[...]
    ms = jnp.mean(x * x, axis=-1, keepdims=True)
    h = x * lax.rsqrt(ms + EPS) * gattn_ref[...]
    h = h * (1.0 + sc_ref[...]) + sh_ref[...]
    hb = h.reshape(tm, d).astype(BF16)

    zq = _dot(hb, wq_ref[...])
    ql = zq * lax.rsqrt(jnp.mean(zq * zq, axis=-1, keepdims=True) + EPS) * gqa_ref[...]
    qm = _dot(ql, wqb_ref[...])
    qn = qm * lax.rsqrt(_group_mean(qm * qm, eq_ref) + EPS) * gq_ref[...]
    wq_lanes = qn.shape[1]
    half = MLA_ROPE // 2
    q_rot = (to3(qn) * tiled(tqc_ref, wq_lanes)
             + to3(pltpu.roll(qn, wq_lanes - half, 1)) * tiled(tqs1_ref, wq_lanes)
             + to3(pltpu.roll(qn, half, 1)) * tiled(tqs2_ref, wq_lanes))
    q_o[...] = q_rot.astype(q_o.dtype)

    zkv = _dot(hb, wkv_ref[...])
    ckv = zkv * lax.rsqrt(jnp.mean(zkv * zkv, axis=-1, keepdims=True) + EPS) * gkva_ref[...]
    ckv_o[...] = to3(ckv)

    zm = _dot(hb, wmisc_ref[...])
    lane = lax.broadcasted_iota(jnp.int32, zm.shape, 1)
    kr = jnp.where(lane < MLA_ROPE, zm, 0.0)
    krn = kr * lax.rsqrt(jnp.sum(kr * kr, axis=-1, keepdims=True) * (1.0 / MLA_ROPE) + EPS) * gkr_ref[...]
    kro = (to3(krn) * tkc_ref[...]
           + to3(pltpu.roll(krn, LANES - half, 1)) * tks1_ref[...]
           + to3(pltpu.roll(krn, half, 1)) * tks2_ref[...])
    krope_o[...] = kro[:, :, :MLA_ROPE]
    logf = _log_sigmoid(zm + bf_ref[...])
    n_fh = logf_o.shape[-1]
    logf_o[...] = to3(pltpu.roll(logf, LANES - LOGF_LANE, 1))[:, :, :n_fh]

    zf = _dot(hb, wf_ref[...])
    fw = zf.shape[1] // 3
    fq_raw, fk_raw, fv = zf[:, :fw], zf[:, fw:2 * fw], zf[:, 2 * fw:]
    fq = fq_raw * lax.rsqrt(_group_mean(fq_raw * fq_raw, ef_ref) + EPS) * gfq_ref[...]
    fk = fk_raw * lax.rsqrt(_group_mean(fk_raw * fk_raw, ef_ref) + EPS) * gfk_ref[...]
    fk_o[...] = to3(fk)
    fv_o[...] = to3(fv)

    if not prompt:
        fq_o[...] = to3(fq).astype(fq_o.dtype)
        return

    ckv_b = ckv.astype(BF16)
    kraw = _dot(ckv_b, wkn_ref[...])
    kn = kraw * lax.rsqrt(_group_mean(kraw * kraw, ek_ref) + EPS) * gk_ref[...]
    kmla = kn + _dot(kro.reshape(tm, LANES), rep_ref[...])
    kmla_o[0] = kmla.astype(kmla_o.dtype)
    ones_col = ones_ref[...]
    vtm_o[0] = (_dot_nt(wvt_ref[...], ckv_b) + ones_col).astype(vtm_o.dtype)
    vtf_o[0] = (_dot_nt(wfvt_ref[...], hb) + ones_col).astype(vtf_o.dtype)

    @pl.when(pl.program_id(1) == 0)
    def _():
        carry_sc[...] = jnp.zeros_like(carry_sc)

    lw = jnp.where((lane >= LOGF_LANE) & (lane < LOGF_LANE + n_fh), logf, 0.0)
    tril = tril_ref[...]
    hi, mid, lo = _split3(lw)
    cum = (jnp.dot(tril, hi, preferred_element_type=F32) + jnp.dot(tril, mid, preferred_element_type=F32)
           + jnp.dot(tril, lo, preferred_element_type=F32)) + carry_sc[...]
    carry_sc[...] = cum[tm - 1:, :]
    hi, mid, lo = _split3(cum * LOG2E)
    parts = (hi.astype(F32) + pltpu.roll(mid.astype(F32), n_fh, 1)
             + pltpu.roll(lo.astype(F32), 2 * n_fh, 1)).astype(BF16)
    fqa = _dot(fq, placef_ref[...]) + _dot(parts, placeq_ref[...]) + constq_ref[...]
    fka = _dot(fk, placef_ref[...]) + _dot(parts, placek_ref[...]) + constk_ref[...]
    fqa_o[0] = fqa.astype(fqa_o.dtype)
    fka_o[0] = fka.astype(fka_o.dtype)


def _token_call(x, shift, scale, consts, prompt_consts, tables, *, prompt, sb, tb):
    nseq, slen, d = x.shape
    grid = (nseq // sb, slen // tb)
    (g_attn, wq, gqa, wqb, gq, eq, wkv, gkva, wmisc, gkr, wf, gfq, gfk, ef, bfv) = consts
    tqc, tqs1, tqs2, tkc, tks1, tks2 = tables

    def tok(width):
        return pl.BlockSpec((sb, tb, width), lambda i, j: (i, j, 0))

    mod_spec = pl.BlockSpec((sb, 1, d), lambda i, j: (i, 0, 0))
    tab_spec = pl.BlockSpec((1, tb, LANES), lambda i, j: (0, j, 0))
    cs = _const_spec
    args = [x, shift, scale, g_attn, wq, gqa, wqb, gq, eq, tqc, tqs1, tqs2,
            wkv, gkva, wmisc, gkr, tkc, tks1, tks2, wf, gfq, gfk, ef, bfv]
    in_specs = [tok(d), mod_spec, mod_spec, cs(g_attn.shape),
                cs(wq.shape), cs(gqa.shape), cs(wqb.shape), cs(gq.shape), cs(eq.shape),
                tab_spec, tab_spec, tab_spec,
                cs(wkv.shape), cs(gkva.shape), cs(wmisc.shape), cs(gkr.shape), tab_spec, tab_spec, tab_spec,
                cs(wf.shape), cs(gfq.shape), cs(gfk.shape), cs(ef.shape), cs(bfv.shape)]
    kv_lora = wkv.shape[1]
    fw = wf.shape[1] // 3
    n_fh = fw // FOX_DIM
    qw = wqb.shape[1]

    def sds(width, dtype):
        return jax.ShapeDtypeStruct((nseq, slen, width), dtype)

    out_shape = [sds(kv_lora, F32), sds(MLA_ROPE, F32), sds(fw, F32), sds(fw, F32), sds(n_fh, F32),
                 sds(qw, BF16)]
    out_specs = [tok(kv_lora), tok(MLA_ROPE), tok(fw), tok(fw), tok(n_fh), tok(qw)]
    scratch = []
    if prompt:
        assert sb == 1
        tril = jnp.asarray(np.tril(np.ones((tb, tb), np.float32)), BF16)
        pc = list(prompt_consts[:8]) + [tril] + list(prompt_consts[8:])
        args += pc
        in_specs += [cs(a.shape) for a in pc]
        vrows = prompt_consts[4].shape[0]
        vt_spec = pl.BlockSpec((1, vrows, tb), lambda i, j: (i, 0, j))
        vt_sds = jax.ShapeDtypeStruct((nseq, vrows, slen), BF16)
        out_shape += [sds(qw, BF16), vt_sds, sds(n_fh * LANES, BF16), sds(n_fh * LANES, BF16), vt_sds]
        out_specs += [tok(qw), vt_spec, tok(n_fh * LANES), tok(n_fh * LANES), vt_spec]
        scratch = [pltpu.VMEM((1, LANES), F32)]
    else:
        out_shape += [sds(fw, BF16)]
        out_specs += [tok(fw)]
    return pl.pallas_call(
        functools.partial(_token_kernel, prompt=prompt),
        grid=grid, in_specs=in_specs, out_specs=out_specs, out_shape=out_shape,
        scratch_shapes=scratch,
        compiler_params=pltpu.CompilerParams(
            dimension_semantics=("parallel", "arbitrary"), vmem_limit_bytes=VMEM_LIMIT),
        name="token_prompt" if prompt else "token_sample",
    )(*args)


def _cumsum_pages_kernel(lt_ref, triu_ref, o_ref):
    r, nh, p = lt_ref.shape
    triu = triu_ref[...]
    hi, mid, lo = _split3(lt_ref[...].reshape(r * nh, p))
    c = (jnp.dot(hi, triu, preferred_element_type=F32)
         + jnp.dot(mid, triu, preferred_element_type=F32)
         + jnp.dot(lo, triu, preferred_element_type=F32))
    o_ref[...] = c.reshape(r, nh, p)


def _cumsum_pages_call(pool_t, rows):
    n_phys, nh, p = pool_t.shape
    triu = jnp.asarray(np.triu(np.ones((p, p), np.float32)), BF16)
    return pl.pallas_call(
        _cumsum_pages_kernel,
        grid=(n_phys // rows,),
        in_specs=[pl.BlockSpec((rows, nh, p), lambda i: (i, 0, 0)), _const_spec((p, p))],
        out_specs=pl.BlockSpec((rows, nh, p), lambda i: (i, 0, 0)),
        out_shape=jax.ShapeDtypeStruct((n_phys, nh, p), F32),
        compiler_params=pltpu.CompilerParams(dimension_semantics=("parallel",)),
        name="cumsum_pages",
    )(pool_t, triu)


def _flash_kernel(qi_ref, ki_ref, q_ref, k_ref, vt_ref, o_ref, m_sc, acc_sc):
    qi = qi_ref[pl.program_id(2)]
    ki = ki_ref[pl.program_id(2)]
    tq = q_ref.shape[1]
    tk = k_ref.shape[1]

    @pl.when(ki == 0)
    def _():
        m_sc[...] = jnp.full_like(m_sc, NEG_INF)
        acc_sc[...] = jnp.zeros_like(acc_sc)

    def step(masked):
        cq = min(FLASH_CQ, tq)
        chains = [(hh, q0) for hh in range(2) for q0 in range(0, tq, cq)]
        n = range(len(chains))
        sts = [_dot_nt(k_ref[0, :, hh * LANES:(hh + 1) * LANES], q_ref[0, q0:q0 + cq, hh * LANES:(hh + 1) * LANES])
               for hh, q0 in chains]
        if masked:
            key = lax.broadcasted_iota(jnp.int32, (tk, cq), 0)
            qry = lax.broadcasted_iota(jnp.int32, (tk, cq), 1)
            sts = [jnp.where(key <= qry + q0, st, NEG_INF) for st, (_, q0) in zip(sts, chains)]
        st3s = [st.reshape(tk // SUBLANES, SUBLANES, cq) for st in sts]
        m_prevs = [m_sc[hh, :, q0:q0 + cq] for hh, q0 in chains]
        m_news = [jnp.maximum(m_prevs[i], jnp.max(jnp.max(st3s[i], axis=0), axis=0, keepdims=True)) for i in n]
        ps = [jnp.exp2(st3s[i] - m_news[i][None]).reshape(tk, cq).astype(BF16) for i in n]
        for i, (hh, q0) in enumerate(chains):
            alpha = jnp.exp2(m_prevs[i] - m_news[i])
            vt = vt_ref[0, hh * V_ROWS:(hh + 1) * V_ROWS, :]
            acc = acc_sc[hh, :, q0:q0 + cq].reshape(V_ROWS // SUBLANES, SUBLANES, cq) * alpha[None]
            acc_sc[hh, :, q0:q0 + cq] = acc.reshape(V_ROWS, cq) + jnp.dot(vt, ps[i], preferred_element_type=F32)
            m_sc[hh, :, q0:q0 + cq] = m_news[i]

    @pl.when(ki < qi)
    def _():
        step(False)

    @pl.when(ki == qi)
    def _():
        step(True)
        outs = []
        for hh in range(2):
            acc = acc_sc[hh]
            outs.append(acc[:MLA_V, :] / acc[MLA_V:MLA_V + 1, :])
        o_ref[0] = jnp.concatenate(outs, axis=0).T.astype(o_ref.dtype)


def _flash_call(q, k, vt, *, tile, name):
    b, s, qw_all = q.shape
    hp = vt.shape[1] // (2 * V_ROWS)
    qw = qw_all // hp
    assert qw == 2 * LANES
    n_t = s // tile
    pairs = [(qi, ki) for qi in range(n_t) for ki in range(qi + 1)]
    qi_tab = jnp.asarray([p[0] for p in pairs], jnp.int32)
    ki_tab = jnp.asarray([p[1] for p in pairs], jnp.int32)
    in_specs = [pl.BlockSpec((1, tile, qw), lambda bi, h, p, qt, kt: (bi, qt[p], h)),
                pl.BlockSpec((1, tile, qw), lambda bi, h, p, qt, kt: (bi, kt[p], h)),
                pl.BlockSpec((1, 2 * V_ROWS, tile), lambda bi, h, p, qt, kt: (bi, h, kt[p]))]
    grid_spec = pltpu.PrefetchScalarGridSpec(
        num_scalar_prefetch=2, grid=(b, hp, len(pairs)), in_specs=in_specs,
        out_specs=pl.BlockSpec((1, tile, LANES), lambda bi, h, p, qt, kt: (bi, qt[p], h)),
        scratch_shapes=[pltpu.VMEM((2, SUBLANES, tile), F32), pltpu.VMEM((2, V_ROWS, tile), F32)])
    return pl.pallas_call(
        _flash_kernel,
        grid_spec=grid_spec,
        out_shape=jax.ShapeDtypeStruct((b, s, hp * LANES), BF16),
        compiler_params=pltpu.CompilerParams(
            dimension_semantics=("parallel", "parallel", "arbitrary"),
            vmem_limit_bytes=VMEM_LIMIT),
        name=name,
    )(qi_tab, ki_tab, q, k, vt)


def _head_rows(x, n_heads, width):
    lane = lax.broadcasted_iota(jnp.int32, x.shape, 1)
    return jnp.concatenate(
        [jnp.where((lane >= h * width) & (lane < (h + 1) * width), x, jnp.zeros_like(x))
         for h in range(n_heads)], axis=0)


def _head_diag(full, n_heads, width, t):
    lane = lax.broadcasted_iota(jnp.int32, (t, full.shape[1]), 1)
    out = jnp.zeros((t, full.shape[1]), full.dtype)
    for h in range(n_heads):
        blk = full[h * t:(h + 1) * t, :]
        out = out + jnp.where((lane >= h * width) & (lane < (h + 1) * width), blk, 0.0)
    return out


def _online_update(s, pv, m_sc, l_sc, acc_sc, first):
    if first:
        m_new = jnp.max(s, axis=1, keepdims=True)
        p = jnp.exp2(s - m_new)
        l_sc[...] = jnp.sum(p, axis=1, keepdims=True)
        acc_sc[...] = pv(p)
    else:
        m_prev = m_sc[...]
        m_new = jnp.maximum(m_prev, jnp.max(s, axis=1, keepdims=True))
        alpha = jnp.exp2(m_prev - m_new)
        p = jnp.exp2(s - m_new)
        l_sc[...] = alpha * l_sc[...] + jnp.sum(p, axis=1, keepdims=True)
        acc_sc[...] = alpha * acc_sc[...] + pv(p)
    m_sc[...] = m_new


def _new_key_mask(rows, t):
    row = lax.broadcasted_iota(jnp.int32, (rows, PAGE), 0)
    col = lax.broadcasted_iota(jnp.int32, (rows, PAGE), 1)
    return col <= row % t


def _pad_rows(x, rows):
    return jnp.concatenate([x, jnp.zeros((rows - x.shape[0], x.shape[1]), x.dtype)], axis=0)


def _mla_decode_kernel(pt_ref, q_ref, cnew_ref, krnew_ref, sel_ref, gkn_ref, wkn_ref, wv_ref, es_ref,
                       *rest, n_pp, n_heads):
    ckv_refs = rest[:n_pp]
    krt_refs = rest[n_pp:2 * n_pp]
    o_ref, qbd_sc, qr_sc, m_sc, l_sc, acc_sc = rest[2 * n_pp:]
    j = pl.program_id(1)
    t = q_ref.shape[1]

    def nope_scores(ckv_b):
        kraw = _dot(ckv_b, wkn_ref[...])
        s = _dot_nt(qbd_sc[...], kraw)
        ssq = _dot_nt(es_ref[...], kraw * kraw)
        return s * lax.rsqrt(ssq + EPS)

    @pl.when(j == 0)
    def _():
        qm = q_ref[0].astype(F32)
        qn = _dot(qm, sel_ref[...]) * gkn_ref[...]
        qbd_sc[...] = _head_rows(qn, n_heads, MLA_NOPE).astype(BF16)
        qr_sc[...] = jnp.concatenate(
            [qm[:, h * LANES + MLA_NOPE:h * LANES + MLA_NOPE + MLA_ROPE] for h in range(n_heads)],
            axis=0).astype(BF16)
        ckv_b = _pad_rows(cnew_ref[0], PAGE).astype(BF16)
        s = nope_scores(ckv_b) + _dot_nt(qr_sc[...], _pad_rows(krnew_ref[0], PAGE))
        s = jnp.where(_new_key_mask(n_heads * t, t), s, NEG_INF)
        _online_update(s, lambda p: _dot(p, ckv_b), m_sc, l_sc, acc_sc, first=True)

    ckv_b = jnp.concatenate([r[0].astype(BF16) for r in ckv_refs], axis=0)
    krt = jnp.concatenate([r[0].astype(BF16) for r in krt_refs], axis=1)
    s = nope_scores(ckv_b) + _dot(qr_sc[...], krt)
    _online_update(s, lambda p: _dot(p, ckv_b), m_sc, l_sc, acc_sc, first=False)

    @pl.when(j == pl.num_programs(1) - 1)
    def _():
        o_lat = acc_sc[...] / l_sc[...]
        full = _dot(o_lat, wv_ref[...])
        o_ref[0] = _head_diag(full, n_heads, MLA_V, t).astype(o_ref.dtype)


def _page_map(slot, n_pp, n_pages, nd):
    def index_map(b, j, pt):
        logical = n_pages - 1 - (j * n_pp + slot)
        return (pt[b * n_pages + logical],) + (0,) * (nd - 1)
    return index_map


def _fox_decode_kernel(pt_ref, q_ref, knew_ref, vnew_ref, lnew_ref, lnewt_ref, *rest, n_pp, n_heads):
    kt_refs = rest[:n_pp]
    vt_refs = rest[n_pp:2 * n_pp]
    c_refs = rest[2 * n_pp:3 * n_pp]
    o_ref, qbd_sc, acol_sc, suf_sc, m_sc, l_sc, acc_sc = rest[3 * n_pp:]
    j = pl.program_id(1)
    t = q_ref.shape[1]

    def head_rows_bias(x):
        return jnp.concatenate(
            [jnp.broadcast_to(x[h:h + 1, :], (t, x.shape[1])) for h in range(n_heads)], axis=0)

    @pl.when(j == 0)
    def _():
        qbd_sc[...] = _head_rows(q_ref[0].astype(F32), n_heads, FOX_DIM).astype(BF16)
        lnew = lnew_ref[0] * LOG2E
        row = lax.broadcasted_iota(jnp.int32, lnew.shape, 0)
        cnew = jnp.zeros_like(lnew)
        for tt in range(t):
            cnew = cnew + jnp.where(row >= tt, lnew[tt:tt + 1, :], 0.0)
        for h in range(n_heads):
            acol_sc[h * t:(h + 1) * t, :] = cnew[:, h:h + 1]
        lnt = lnewt_ref[0] * LOG2E
        lane = lax.broadcasted_iota(jnp.int32, lnt.shape, 1)
        cnew_t = jnp.zeros_like(lnt)
        for tt in range(t):
            cnew_t = cnew_t + jnp.where(lane >= tt, lnt[:, tt:tt + 1], 0.0)
        suf_sc[...] = jnp.zeros_like(suf_sc)
        k_b = _pad_rows(knew_ref[0], PAGE).astype(BF16)
        v_b = _pad_rows(vnew_ref[0], PAGE).astype(BF16)
        s = _dot_nt(qbd_sc[...], k_b) + (acol_sc[...] - head_rows_bias(cnew_t))
        s = jnp.where(_new_key_mask(n_heads * t, t), s, NEG_INF)
        _online_update(s, lambda p: _dot(p, v_b), m_sc, l_sc, acc_sc, first=True)

    suffix = suf_sc[...]
    biases = []
    for p in range(n_pp):
        c = c_refs[p][0] * LOG2E
        total = c[:, PAGE - 1:PAGE]
        biases.append(suffix + (total - c))
        suffix = suffix + total
    suf_sc[...] = suffix
    bias = biases[0] if n_pp == 1 else jnp.concatenate(biases, axis=1)
    kt_b = jnp.concatenate([r[0].astype(BF16) for r in kt_refs], axis=1)
    vt_b = jnp.concatenate([r[0].astype(BF16) for r in vt_refs], axis=1)
    s = _dot(qbd_sc[...], kt_b) + (acol_sc[...] + head_rows_bias(bias))
    _online_update(s, lambda p: _dot_nt(p, vt_b), m_sc, l_sc, acc_sc, first=False)

    @pl.when(j == pl.num_programs(1) - 1)
    def _():
        o_ref[0] = _head_diag(acc_sc[...] / l_sc[...], n_heads, FOX_DIM, t).astype(o_ref.dtype)


N_MLA_FIXED = 8
N_FOX_FIXED = 5
N_MLA_SCRATCH = 5


def _decode_kernel(pt_ref, *refs, n_pp, n_heads):
    n_mla = N_MLA_FIXED + 2 * n_pp
    n_fox = N_FOX_FIXED + 3 * n_pp
    mla_in, fox_in = refs[:n_mla], refs[n_mla:n_mla + n_fox]
    o_mla, o_fox = refs[n_mla + n_fox:n_mla + n_fox + 2]
    scratch = refs[n_mla + n_fox + 2:]
    _mla_decode_kernel(pt_ref, *mla_in, o_mla, *scratch[:N_MLA_SCRATCH], n_pp=n_pp, n_heads=n_heads)
    _fox_decode_kernel(pt_ref, *fox_in, o_fox, *scratch[N_MLA_SCRATCH:], n_pp=n_pp, n_heads=n_heads)


def _decode_call(pt_flat, mla_args, mla_pools, fox_args, fox_pools, *, n_pp):
    q, ckv_new, kr_new, sel, gkn, wkn, wv, es = mla_args
    pool_ckv, pool_krt = mla_pools
    fq, k_new, v_new, l_new, l_new_t = fox_args
    pool_kt, pool_vt, pool_c = fox_pools
    bd, t, qw = q.shape
    fw = fq.shape[2]
    n_pages = pt_flat.shape[0] // bd
    n_heads = qw // LANES
    assert n_heads == fw // FOX_DIM
    kv_lora = pool_ckv.shape[-1]
    ow = wv.shape[1]
    rows = n_heads * t

    def batch(shape):
        return pl.BlockSpec((1,) + shape, lambda b, j, pt: (b, 0, 0))

    def pages(shape):
        return [pl.BlockSpec((1,) + shape, _page_map(p, n_pp, n_pages, 3)) for p in range(n_pp)]

    cs = _const_spec
    in_specs = [batch((t, qw)), batch((t, kv_lora)), batch((t, MLA_ROPE)),
                cs(sel.shape), cs(gkn.shape), cs(wkn.shape), cs(wv.shape), cs(es.shape)]
    in_specs += pages((PAGE, kv_lora)) + pages((MLA_ROPE, PAGE))
    in_specs += [batch((t, fw)), batch((t, fw)), batch((t, fw)), batch((t, n_heads)), batch((n_heads, PAGE))]
    in_specs += pages((fw, PAGE)) + pages((fw, PAGE)) + pages((n_heads, PAGE))
    assert len(mla_args) == N_MLA_FIXED and len(fox_args) == N_FOX_FIXED
    grid_spec = pltpu.PrefetchScalarGridSpec(
        num_scalar_prefetch=1, grid=(bd, n_pages // n_pp), in_specs=in_specs,
        out_specs=[pl.BlockSpec((1, t, ow), lambda b, j, pt: (b, 0, 0)),
                   pl.BlockSpec((1, t, fw), lambda b, j, pt: (b, 0, 0))],
        scratch_shapes=[pltpu.VMEM((rows, wkn.shape[1]), BF16), pltpu.VMEM((rows, MLA_ROPE), BF16),
                        pltpu.VMEM((rows, 1), F32), pltpu.VMEM((rows, 1), F32),
                        pltpu.VMEM((rows, kv_lora), F32),
                        pltpu.VMEM((rows, fw), BF16), pltpu.VMEM((rows, 1), F32),
                        pltpu.VMEM((n_heads, 1), F32),
                        pltpu.VMEM((rows, 1), F32), pltpu.VMEM((rows, 1), F32),
                        pltpu.VMEM((rows, fw), F32)])
    return pl.pallas_call(
        functools.partial(_decode_kernel, n_pp=n_pp, n_heads=n_heads),
        grid_spec=grid_spec,
        out_shape=[jax.ShapeDtypeStruct((bd, t, ow), BF16), jax.ShapeDtypeStruct((bd, t, fw), BF16)],
        compiler_params=pltpu.CompilerParams(
            dimension_semantics=("parallel", "arbitrary"), vmem_limit_bytes=VMEM_LIMIT),
        name="paged_decode",
    )(pt_flat, *mla_args, *([pool_ckv] * n_pp), *([pool_krt] * n_pp),
      *fox_args, *([pool_kt] * n_pp), *([pool_vt] * n_pp), *([pool_c] * n_pp))


def _out_mlp_kernel(x_ref, om_ref, of_ref, ga_ref, shm_ref, scm_ref, gm_ref, gmlp_ref,
                    wom_ref, wof_ref, w1_ref, w2_ref, y_ref, *, ff_chunk):
    sb, tb, d = x_ref.shape
    tm = sb * tb
    o = (_dot(om_ref[...].reshape(tm, om_ref.shape[-1]), wom_ref[...])
         + _dot(of_ref[...].reshape(tm, of_ref.shape[-1]), wof_ref[...]))
    x1 = x_ref[...] + ga_ref[...] * o.reshape(sb, tb, d)
    ms = jnp.mean(x1 * x1, axis=-1, keepdims=True)
    h = x1 * lax.rsqrt(ms + EPS) * gmlp_ref[...]
    h = h * (1.0 + scm_ref[...]) + shm_ref[...]
    hb = h.reshape(tm, d).astype(BF16)
    d_ff = w1_ref.shape[1]
    acc = jnp.zeros((tm, d), F32)
    for c in range(d_ff // ff_chunk):
        u = jnp.maximum(_dot(hb, w1_ref[:, c * ff_chunk:(c + 1) * ff_chunk]), 0.0)
        acc = acc + _dot(u * u, w2_ref[c * ff_chunk:(c + 1) * ff_chunk, :])
    y_ref[...] = x1 + gm_ref[...] * acc.reshape(sb, tb, d)


def _out_mlp_call(x, o_mla, o_fox, gate_a, shift_m, scale_m, gate_m, g_mlp, wom, wof, w1, w2, *, sb, tb):
    nseq, slen, d = x.shape

    def tok(width):
        return pl.BlockSpec((sb, tb, width), lambda i, j: (i, j, 0))

    mod_spec = pl.BlockSpec((sb, 1, d), lambda i, j: (i, 0, 0))

    def weight(shape):
        return pl.BlockSpec(shape, lambda i, j: (0, 0), pipeline_mode=pl.Buffered(1))

    return pl.pallas_call(
        functools.partial(_out_mlp_kernel, ff_chunk=min(1024, w1.shape[1])),
        grid=(nseq // sb, slen // tb),
        in_specs=[tok(d), tok(o_mla.shape[-1]), tok(o_fox.shape[-1]),
                  mod_spec, mod_spec, mod_spec, mod_spec, _const_spec(g_mlp.shape),
                  weight(wom.shape), weight(wof.shape), weight(w1.shape), weight(w2.shape)],
        out_specs=tok(d),
        out_shape=jax.ShapeDtypeStruct((nseq, slen, d), F32),
        compiler_params=pltpu.CompilerParams(
            dimension_semantics=("parallel", "parallel"), vmem_limit_bytes=VMEM_LIMIT),
        name="out_mlp",
    )(x, o_mla, o_fox, gate_a, shift_m, scale_m, gate_m, g_mlp, wom, wof, w1, w2)


def _rope_tables(pos, scale_q):
    half = MLA_ROPE // 2
    inv_freq = ROPE_THETA ** (-jnp.arange(half, dtype=F32) / half)
    ang = pos.astype(F32)[:, None] * inv_freq[None, :]
    cos, sin = jnp.cos(ang), jnp.sin(ang)
    n = pos.shape[0]
    z = lambda w: jnp.zeros((n, w), F32)
    one = jnp.ones((n, MLA_NOPE), F32)
    tqc = jnp.concatenate([one, cos, cos, z(LANES - MLA_NOPE - MLA_ROPE)], axis=1) * scale_q
    tqs1 = jnp.concatenate([z(MLA_NOPE), -sin, z(LANES - MLA_NOPE - half)], axis=1) * scale_q
    tqs2 = jnp.concatenate([z(MLA_NOPE + half), sin, z(LANES - MLA_NOPE - MLA_ROPE)], axis=1) * scale_q
    tkc = jnp.concatenate([cos, cos, z(LANES - MLA_ROPE)], axis=1)
    tks1 = jnp.concatenate([-sin, z(LANES - half)], axis=1)
    tks2 = jnp.concatenate([z(half), sin, z(LANES - MLA_ROPE)], axis=1)
    return tuple(t[None] for t in (tqc, tqs1, tqs2, tkc, tks1, tks2))


def _layer_consts(w_in, b_f, g_attn, g_qa, w_qb, g_qn, g_qr, g_kva, g_kr, w_kvb, g_kn, g_fq, g_fk):
    d = w_in.shape[0]
    q_lora, n_mh, _ = w_qb.shape
    kv_lora = w_kvb.shape[0]
    n_fh = b_f.shape[0]
    fw = n_fh * FOX_DIM
    c0, c1, c2 = q_lora, q_lora + kv_lora, q_lora + kv_lora + MLA_ROPE
    c3 = c2 + 3 * fw
    pad_h = LANES - MLA_NOPE - MLA_ROPE
    row = lambda v: v.reshape(1, -1).astype(F32)

    wq = w_in[:, :c0].astype(BF16)
    wkv = w_in[:, c0:c1].astype(BF16)
    wmisc = jnp.concatenate([w_in[:, c1:c2], w_in[:, c3:c3 + n_fh],
                             jnp.zeros((d, LANES - MLA_ROPE - n_fh), F32)], axis=1).astype(BF16)
    wf = w_in[:, c2:c3].astype(BF16)
    wqb = jnp.pad(w_qb, ((0, 0), (0, 0), (0, pad_h))).reshape(q_lora, n_mh * LANES).astype(BF16)
    gq = row(jnp.concatenate([g_qn, g_qr, jnp.zeros((n_mh, pad_h), F32)], axis=1))
    wkn = jnp.pad(w_kvb[:, :, :MLA_NOPE], ((0, 0), (0, 0), (0, LANES - MLA_NOPE)))
    wkn = wkn.reshape(kv_lora, n_mh * LANES).astype(BF16)
    gk = row(jnp.pad(g_kn, ((0, 0), (0, LANES - MLA_NOPE))))
    wkn_c = w_kvb[:, :, :MLA_NOPE].reshape(kv_lora, n_mh * MLA_NOPE).astype(BF16)
    wv_c = w_kvb[:, :, MLA_NOPE:].reshape(kv_lora, n_mh * MLA_V).astype(BF16)
    gkr = row(jnp.pad(g_kr, (0, LANES - MLA_ROPE)))
    bfv = row(jnp.concatenate([jnp.zeros((LOGF_LANE,), F32), b_f, jnp.zeros((LANES - LOGF_LANE - n_fh,), F32)]))
    gfq = row(g_fq) * (FOX_DIM ** -0.5 * LOG2E)
    gfk = row(g_fk)

    pad_v = V_ROWS - MLA_V
    wvt = jnp.pad(jnp.transpose(w_kvb[:, :, MLA_NOPE:], (1, 2, 0)), ((0, 0), (0, pad_v), (0, 0)))
    wvt = wvt.reshape(n_mh * V_ROWS, kv_lora).astype(BF16)
    wfv = w_in[:, c2 + 2 * fw:c3].reshape(d, n_fh, FOX_DIM)
    wfvt = jnp.pad(jnp.transpose(wfv, (1, 2, 0)), ((0, 0), (0, pad_v), (0, 0)))
    wfvt = wfvt.reshape(n_fh * V_ROWS, d).astype(BF16)
    assert n_mh == n_fh
    ones_col = np.zeros((n_mh, V_ROWS, 1), np.float32)
    ones_col[:, MLA_V:, :] = 1.0
    ones_col = jnp.asarray(ones_col.reshape(n_mh * V_ROWS, 1))

    idx = np.arange(LANES)
    nope = idx < MLA_NOPE
    rope = (idx >= MLA_NOPE) & (idx < MLA_NOPE + MLA_ROPE)
    eq = (np.outer(nope, nope) / MLA_NOPE + np.outer(rope, rope) / MLA_ROPE).astype(np.float32)
    ek = (np.outer(nope, nope) / MLA_NOPE).astype(np.float32)
    ef = ((idx[:, None] // FOX_DIM) == (idx[None, :] // FOX_DIM)).astype(np.float32) / FOX_DIM
    rep = np.zeros((LANES, n_mh * LANES), np.float32)
    sel = np.zeros((n_mh * LANES, n_mh * MLA_NOPE), np.float32)
    placef = np.zeros((fw, n_fh * LANES), np.float32)
    placeq = np.zeros((LANES, n_fh * LANES), np.float32)
    placek = np.zeros((LANES, n_fh * LANES), np.float32)
    constq = np.zeros((1, n_fh * LANES), np.float32)
    constk = np.zeros((1, n_fh * LANES), np.float32)
    for h in range(n_mh):
        rep[np.arange(MLA_ROPE), h * LANES + MLA_NOPE + np.arange(MLA_ROPE)] = 1.0
        sel[h * LANES + np.arange(MLA_NOPE), h * MLA_NOPE + np.arange(MLA_NOPE)] = 1.0
    for h in range(n_fh):
        placef[h * FOX_DIM + np.arange(FOX_DIM), h * LANES + np.arange(FOX_DIM)] = 1.0
        for part in range(3):
            placeq[LOGF_LANE + part * n_fh + h, h * LANES + BIAS_LANE + part] = 1.0
            placek[LOGF_LANE + part * n_fh + h, h * LANES + BIAS_LANE + 3 + part] = -1.0
            constq[0, h * LANES + BIAS_LANE + 3 + part] = 1.0
            constk[0, h * LANES + BIAS_LANE + part] = 1.0
    t_dec = SUBLANES
    es = np.zeros((n_mh * t_dec, n_mh * MLA_NOPE), np.float32)
    for h in range(n_mh):
        es[h * t_dec:(h + 1) * t_dec, h * MLA_NOPE:(h + 1) * MLA_NOPE] = 1.0 / MLA_NOPE

    bf = lambda a: jnp.asarray(a, BF16)
    token_consts = (row(g_attn), wq, row(g_qa), wqb, gq, bf(eq), wkv, row(g_kva),
                    wmisc, gkr, wf, gfq, gfk, bf(ef), bfv)
    prompt_consts = (wkn, gk, bf(ek), bf(rep), wvt, wfvt, ones_col, bf(placef),
                     bf(placeq), bf(placek), jnp.asarray(constq), jnp.asarray(constk))
    decode_consts = (bf(sel), row(g_kn), wkn_c, wv_c, bf(es))
    return token_consts, prompt_consts, decode_consts


def kernel(x_prompt, x_sample, cache_mla_ckv, cache_mla_krope, cache_fox_k, cache_fox_v, cache_fox_logf,
           page_table, c_prompt, c_sample, w_ada, b_ada, g_attn, g_mlp, w_in, b_f, g_qa, w_qb, g_qn,
           g_qr, g_kva, g_kr, w_kvb, g_kn, g_fq, g_fk, w_o, w_mlp1, w_mlp2):
    depth = w_ada.shape[0]
    b, s, d = x_prompt.shape
    bd, t_dec, _ = x_sample.shape
    assert t_dec == SUBLANES, "decode kernels lay the new tokens on one sublane tile"
    n_pages = page_table.shape[1]
    past_len = n_pages * PAGE
    n_phys = cache_mla_ckv.shape[1]
    n_fh = b_f.shape[1]
    fw = n_fh * FOX_DIM
    n_mh = w_qb.shape[2]
    assert n_fh % 2 == 0 and n_mh % 2 == 0
    assert 3 * n_fh <= LANES - LOGF_LANE

    tile_p = min(512, s)
    sb_s = min(64, bd)
    n_pp = min(16, n_pages)
    cum_rows = math.gcd(n_phys, 256)
    assert s % tile_p == 0 and bd % sb_s == 0 and n_pages % n_pp == 0

    scale_mla = (MLA_NOPE + MLA_ROPE) ** -0.5 * LOG2E
    tabs_p = _rope_tables(jnp.arange(s, dtype=jnp.int32), scale_mla)
    tabs_s = _rope_tables(past_len + jnp.arange(t_dec, dtype=jnp.int32), scale_mla)
    pt_flat = page_table.reshape(-1).astype(jnp.int32)

    y_p, y_s = x_prompt, x_sample
    rows_p, rows_s = [], []
    for l in range(depth):
        token_consts, prompt_consts, decode_consts = _layer_consts(
            w_in[l], b_f[l], g_attn[l], g_qa[l], w_qb[l], g_qn[l], g_qr[l], g_kva[l], g_kr[l],
            w_kvb[l], g_kn[l], g_fq[l], g_fk[l])
        mod = _ada_call(jnp.concatenate([c_prompt, c_sample], axis=0), w_ada[l], b_ada[l])
        mods_p = [m[:, None, :] for m in jnp.split(mod[:b], 6, axis=-1)]
        mods_s = [m[:, None, :] for m in jnp.split(mod[b:], 6, axis=-1)]

        (p_ckv, p_krope, p_k, p_v, p_logf, q_mla, k_mla, vt_mla, fq_aug, fk_aug, vt_fox) = _token_call(
            y_p, mods_p[0], mods_p[1], token_consts, prompt_consts, tabs_p, prompt=True, sb=1, tb=tile_p)
        (s_ckv, s_krope, s_k, s_v, s_logf, q_mla_s, fq_s) = _token_call(
            y_s, mods_s[0], mods_s[1], token_consts, None, tabs_s, prompt=False, sb=sb_s, tb=t_dec)

        o_mla_p = _flash_call(q_mla, k_mla, vt_mla, tile=tile_p, name="mla_prompt")
        o_fox_p = _flash_call(fq_aug, fk_aug, vt_fox, tile=tile_p, name="fox_prompt")

        sel, gkn, wkn_c, wv_c, es = decode_consts
        pool_krt = jnp.swapaxes(cache_mla_krope[l], 1, 2)
        pool_kt = jnp.transpose(cache_fox_k[l], (0, 2, 3, 1)).reshape(n_phys, fw, PAGE)
        pool_vt = jnp.transpose(cache_fox_v[l], (0, 2, 3, 1)).reshape(n_phys, fw, PAGE)
        pool_c = _cumsum_pages_call(jnp.swapaxes(cache_fox_logf[l], 1, 2), cum_rows)
        l_new_t = jnp.pad(jnp.swapaxes(s_logf, 1, 2), ((0, 0), (0, 0), (0, PAGE - t_dec)))
        o_mla_s, o_fox_s = _decode_call(
            pt_flat, (q_mla_s, s_ckv, s_krope, sel, gkn, wkn_c, wv_c, es), (cache_mla_ckv[l], pool_krt),
            (fq_s, s_k, s_v, s_logf, l_new_t), (pool_kt, pool_vt, pool_c), n_pp=n_pp)

        wo = w_o[l].astype(BF16)
        wom, wof = wo[:n_mh * MLA_V], wo[n_mh * MLA_V:]
        w1 = w_mlp1[l].astype(BF16)
        w2 = w_mlp2[l].astype(BF16)
        g_mlp_row = g_mlp[l].reshape(1, -1)
        y_p = _out_mlp_call(y_p, o_mla_p, o_fox_p, mods_p[2], mods_p[3], mods_p[4], mods_p[5],
                            g_mlp_row, wom, wof, w1, w2, sb=1, tb=tile_p)
        y_s = _out_mlp_call(y_s, o_mla_s, o_fox_s, mods_s[2], mods_s[3], mods_s[4], mods_s[5],
                            g_mlp_row, wom, wof, w1, w2, sb=sb_s, tb=t_dec)

        rows_p.append((p_ckv, p_krope, p_k.reshape(b, s, n_fh, FOX_DIM), p_v.reshape(b, s, n_fh, FOX_DIM), p_logf))
        rows_s.append((s_ckv, s_krope, s_k.reshape(bd, t_dec, n_fh, FOX_DIM),
                       s_v.reshape(bd, t_dec, n_fh, FOX_DIM), s_logf))

    outs_p = [jnp.stack(t, axis=0) for t in zip(*rows_p)]
    outs_s = [jnp.stack(t, axis=0) for t in zip(*rows_s)]
    return (y_p, y_s, *outs_p, *outs_s)
```
